```python
import jax, jax.numpy as jnp
from jax import lax
import numpy as np

D_MODEL = 1024
BATCH = 4
SEQ = 8192
DEPTH = 2

M_HEADS = 4
M_HEAD_DIM = D_MODEL // 8
M_WIDTH = M_HEADS * M_HEAD_DIM
A_HEADS = 8
A_HEAD_DIM = D_MODEL // 16
A_WIDTH = A_HEADS * A_HEAD_DIM
D_MIX = M_WIDTH + A_WIDTH
CONV_K = 4
MLSTM_CHUNK = 64
GATE_SOFTCAP = 15.0
MOBA_BLOCK = 256
MOBA_TOPK = 3
MOBA_Q_BLOCK = 64
ROPE_THETA = 500000.0
ROPE_DIM = A_HEAD_DIM // 4
N_EXPERTS = 32
TOP_K = 4
D_FF = D_MODEL
SWIGLU_LIMIT = 7.0
SWIGLU_ALPHA = 1.702
EXPERT_ROWS = 128
NORM_EPS = 1e-5
IN_SPLITS = (M_WIDTH, M_WIDTH, M_WIDTH, M_WIDTH, M_HEADS, M_HEADS, A_WIDTH, A_WIDTH, A_WIDTH)
N_IN = sum(IN_SPLITS)

kernel_name = 'hybrid_mlstm_moba_moe_adaln'


def rms_norm(x, g):
    xf = x.astype(jnp.float32)
    y = xf * lax.rsqrt(jnp.mean(xf * xf, axis=-1, keepdims=True) + NORM_EPS)
    return (y * g.astype(jnp.float32)).astype(x.dtype)


def soft_cap(t):
    t = t.astype(jnp.float32)
    return GATE_SOFTCAP * jnp.tanh(t / GATE_SOFTCAP)


def causal_depthwise_conv(x, w, b):
    y = lax.conv_general_dilated(x, w[:, None, :].astype(x.dtype), window_strides=(1,),
                                 padding=[(CONV_K - 1, 0)],
                                 dimension_numbers=('NWC', 'WIO', 'NWC'),
                                 feature_group_count=x.shape[-1])
    return y + b


def partial_rotary(x, positions):
    half = ROPE_DIM // 2
    inv_freq = ROPE_THETA ** (-jnp.arange(0, ROPE_DIM, 2, dtype=jnp.float32) / ROPE_DIM)
    ang = positions.astype(jnp.float32)[:, None, :, None] * inv_freq
    cos, sin = jnp.cos(ang), jnp.sin(ang)
    xr = x[..., :ROPE_DIM].astype(jnp.float32)
    x1, x2 = xr[..., :half], xr[..., half:]
    rot = jnp.concatenate([x1 * cos - x2 * sin, x2 * cos + x1 * sin], axis=-1)
    return jnp.concatenate([rot.astype(x.dtype), x[..., ROPE_DIM:]], axis=-1)


def mlstm_chunkwise(q, k, v, i_pre, f_pre):
    B, H, S, d = q.shape
    L = MLSTM_CHUNK
    nc = S // L
    f32 = jnp.float32
    q = q.astype(f32).reshape(B, H, nc, L, d)
    k = (k.astype(f32) * d ** -0.5).reshape(B, H, nc, L, d)
    v = v.astype(f32).reshape(B, H, nc, L, d)
    ig = i_pre.astype(f32).reshape(B, H, nc, L)
    b = jnp.cumsum(jax.nn.log_sigmoid(f_pre.astype(f32)).reshape(B, H, nc, L), axis=-1)
    g = b[..., -1]
    w = g[..., None] - b + ig
    wmax = jnp.max(w, axis=-1)
    ew = jnp.exp(w - wmax[..., None])
    kv = jnp.einsum('bhcl,bhcld,bhcle->cbhde', ew, k, v)
    ksum = jnp.einsum('bhcl,bhcld->cbhd', ew, k)

    def step(carry, inp):
        c_st, n_st, m_st = carry
        kv_c, ks_c, g_c, wm_c = inp
        m_new = jnp.maximum(g_c + m_st, wm_c)
        decay = jnp.exp(g_c + m_st - m_new)
        inj = jnp.exp(wm_c - m_new)
        c_new = decay[..., None, None] * c_st + inj[..., None, None] * kv_c
        n_new = decay[..., None] * n_st + inj[..., None] * ks_c
        return (c_new, n_new, m_new), (c_st, n_st, m_st)

    init = (jnp.zeros((B, H, d, d), f32), jnp.zeros((B, H, d), f32), jnp.full((B, H), -jnp.inf, f32))
    _, (c_prev, n_prev, m_prev) = lax.scan(step, init, (kv, ksum, g.transpose(2, 0, 1), wmax.transpose(2, 0, 1)))
    m_prev = m_prev.transpose(1, 2, 0)
    a = b + m_prev[..., None]
    causal = jnp.tril(jnp.ones((L, L), dtype=bool))
    dmat = jnp.where(causal, b[..., :, None] - b[..., None, :] + ig[..., None, :], -jnp.inf)
    m_t = jnp.maximum(a, jnp.max(dmat, axis=-1))
    s = jnp.einsum('bhctd,bhcsd->bhcts', q, k) * jnp.exp(dmat - m_t[..., None])
    inter = jnp.exp(a - m_t)
    num = jnp.einsum('bhcts,bhcse->bhcte', s, v) + inter[..., None] * jnp.einsum('bhctd,cbhde->bhcte', q, c_prev)
    den = jnp.sum(s, axis=-1) + inter * jnp.einsum('bhctd,cbhd->bhct', q, n_prev)
    h = num / jnp.maximum(jnp.abs(den), jnp.exp(-m_t))[..., None]
    return h.reshape(B, H, S, d)


def moba_attention(q, k, v):
    B, H, S, dh = q.shape
    nb = -(-S // MOBA_BLOCK)
    pad = nb * MOBA_BLOCK - S
    kp = jnp.pad(k, ((0, 0), (0, 0), (0, pad), (0, 0)))
    vp = jnp.pad(v, ((0, 0), (0, 0), (0, pad), (0, 0)))
    kb = kp.reshape(B, H, nb, MOBA_BLOCK, dh)
    vb = vp.reshape(B, H, nb, MOBA_BLOCK, dh)
    kmean = jnp.mean(kb.astype(jnp.float32), axis=3)
    n_sel = min(MOBA_TOPK, nb)
    scale = dh ** -0.5
    bi = jnp.arange(B)[:, None, None, None]
    hi = jnp.arange(H)[None, :, None, None]
    blk_ids = jnp.arange(nb)

    def one_query_block(ci):
        start = ci * MOBA_Q_BLOCK
        own = start // MOBA_BLOCK
        qb = lax.dynamic_slice_in_dim(q, start, MOBA_Q_BLOCK, axis=2).astype(jnp.float32)
        gate = jnp.einsum('bhqd,bhnd->bhqn', qb, kmean)
        gate = jnp.where(blk_ids < own, gate, -jnp.inf)
        topv, topi = lax.top_k(gate, n_sel)
        sel_ok = topv > -jnp.inf
        kg = kb[bi, hi, topi]
        vg = vb[bi, hi, topi]
        s_sel = jnp.einsum('bhqd,bhqjkd->bhqjk', qb, kg) * scale
        s_sel = jnp.where(sel_ok[..., None], s_sel, -jnp.inf).reshape(B, H, MOBA_Q_BLOCK, n_sel * MOBA_BLOCK)
        kown = lax.dynamic_slice_in_dim(kp, own * MOBA_BLOCK, MOBA_BLOCK, axis=2)
        vown = lax.dynamic_slice_in_dim(vp, own * MOBA_BLOCK, MOBA_BLOCK, axis=2)
        s_own = jnp.einsum('bhqd,bhkd->bhqk', qb, kown) * scale
        qpos = start + jnp.arange(MOBA_Q_BLOCK)
        kpos = own * MOBA_BLOCK + jnp.arange(MOBA_BLOCK)
        s_own = jnp.where(kpos[None, :] <= qpos[:, None], s_own, -jnp.inf)
        p = jax.nn.softmax(jnp.concatenate([s_sel, s_own], axis=-1), axis=-1)
        p_sel = p[..., :n_sel * MOBA_BLOCK].reshape(B, H, MOBA_Q_BLOCK, n_sel, MOBA_BLOCK)
        out = jnp.einsum('bhqjk,bhqjkd->bhqd', p_sel, vg) + jnp.einsum('bhqk,bhkd->bhqd', p[..., n_sel * MOBA_BLOCK:], vown)
        return out.astype(q.dtype)

    out = lax.map(one_query_block, jnp.arange(S // MOBA_Q_BLOCK))
    return out.transpose(1, 2, 0, 3, 4).reshape(B, H, S, dh)


def hybrid_mixer(h, positions, w_in, conv_w, conv_b, b_igate, b_fgate, mlstm_norm_g, w_out):
    B, S, _ = h.shape
    offsets = np.cumsum(IN_SPLITS)[:-1].tolist()
    z = h @ w_in
    mq, mk, mv, mo, mi, mf, aq, ak, av = jnp.split(z, offsets, axis=-1)

    def heads(t, n, d):
        return t.reshape(B, S, n, d).transpose(0, 2, 1, 3)

    qk = jax.nn.silu(causal_depthwise_conv(jnp.concatenate([mq, mk], axis=-1), conv_w, conv_b))
    mq, mk = jnp.split(qk, 2, axis=-1)
    i_pre = soft_cap(mi + b_igate).transpose(0, 2, 1)
    f_pre = soft_cap(mf + b_fgate).transpose(0, 2, 1)
    hm = mlstm_chunkwise(heads(mq, M_HEADS, M_HEAD_DIM), heads(mk, M_HEADS, M_HEAD_DIM),
                         heads(mv, M_HEADS, M_HEAD_DIM), i_pre, f_pre)
    hm = hm.transpose(0, 2, 1, 3) * jax.nn.sigmoid(mo.astype(jnp.float32)).reshape(B, S, M_HEADS, M_HEAD_DIM)
    hm = rms_norm(hm, mlstm_norm_g.reshape(M_HEADS, M_HEAD_DIM)).reshape(B, S, M_WIDTH).astype(h.dtype)

    qa = partial_rotary(heads(aq, A_HEADS, A_HEAD_DIM), positions)
    ka = partial_rotary(heads(ak, A_HEADS, A_HEAD_DIM), positions)
    ha = moba_attention(qa, ka, heads(av, A_HEADS, A_HEAD_DIM))
    ha = ha.transpose(0, 2, 1, 3).reshape(B, S, A_WIDTH)

    return jnp.concatenate([hm, ha], axis=-1) @ w_out


def moe_ffn(h, w_router, b_router, w_gate, b_gate, w_up, b_up, w_down, b_down):
    B, S, D = h.shape
    T = B * S
    A = T * TOP_K
    xt = h.reshape(T, D)
    logits = (xt @ w_router + b_router).astype(jnp.float32)
    topv, topi = lax.top_k(logits, TOP_K)
    probs = jax.nn.softmax(topv, axis=-1)
    flat_e = topi.reshape(A)
    flat_tok = jnp.arange(A, dtype=jnp.int32) // TOP_K
    flat_p = probs.reshape(A)
    order = jnp.argsort(flat_e)
    se, stok, sp = flat_e[order], flat_tok[order], flat_p[order]
    counts = jnp.bincount(flat_e, length=N_EXPERTS)
    start = jnp.cumsum(counts) - counts
    nblk = (counts + EXPERT_ROWS - 1) // EXPERT_ROWS
    blk_end = jnp.cumsum(nblk)
    blk_start = blk_end - nblk
    slot = blk_start[se] * EXPERT_ROWS + (jnp.arange(A) - start[se])
    n_blocks = (A + EXPERT_ROWS - 1) // EXPERT_ROWS + N_EXPERTS
    P = n_blocks * EXPERT_ROWS
    slot_tok = jnp.zeros((P,), jnp.int32).at[slot].set(stok)
    slot_p = jnp.zeros((P,), jnp.float32).at[slot].set(sp)
    block_e = jnp.minimum(jnp.searchsorted(blk_end, jnp.arange(n_blocks), side='right'), N_EXPERTS - 1)

    def expert_block(args):
        tok, e = args
        xb = xt[tok]
        gt = jnp.minimum(xb @ w_gate[e] + b_gate[e], SWIGLU_LIMIT)
        up = jnp.clip(xb @ w_up[e] + b_up[e], -SWIGLU_LIMIT, SWIGLU_LIMIT)
        act = (up + 1.0) * (gt * jax.nn.sigmoid(SWIGLU_ALPHA * gt))
        return act @ w_down[e] + b_down[e]

    yb = lax.map(expert_block, (slot_tok.reshape(n_blocks, EXPERT_ROWS), block_e))
    yb = yb.reshape(P, D) * slot_p[:, None].astype(yb.dtype)
    y = jnp.zeros((T, D), yb.dtype).at[slot_tok].add(yb)
    return y.reshape(B, S, D).astype(h.dtype)


def setup_inputs(seed: int = 0) -> dict:
    key = jax.random.key(seed)
    ks = jax.random.split(key, 24)
    f32 = jnp.float32

    def nrm(k, shape, scale):
        return jax.random.normal(k, shape, f32) * scale

    x = nrm(ks[0], (BATCH, SEQ, D_MODEL), 1.0)
    c = nrm(ks[1], (BATCH, D_MODEL), 1.0)
    positions = jnp.arange(SEQ, dtype=jnp.int32)[None, :] + jax.random.randint(ks[2], (BATCH, 1), 0, 4096, dtype=jnp.int32)
    w_ada = nrm(ks[3], (DEPTH, D_MODEL, 6 * D_MODEL), 0.5 * D_MODEL ** -0.5)
    b_ada = nrm(ks[4], (DEPTH, 6 * D_MODEL), 0.02)
    norm1_g = 1.0 + nrm(ks[5], (DEPTH, D_MODEL), 0.05)
    w_in = nrm(ks[6], (DEPTH, D_MODEL, N_IN), D_MODEL ** -0.5)
    conv_w = nrm(ks[7], (DEPTH, CONV_K, 2 * M_WIDTH), CONV_K ** -0.5)
    conv_b = nrm(ks[8], (DEPTH, 2 * M_WIDTH), 0.02)
    b_igate = nrm(ks[9], (DEPTH, M_HEADS), 0.1)
    b_fgate = jnp.linspace(3.0, 6.0, M_HEADS, dtype=f32)[None, :] + nrm(ks[10], (DEPTH, M_HEADS), 0.1)
    mlstm_norm_g = 1.0 + nrm(ks[11], (DEPTH, M_WIDTH), 0.05)
    w_out = nrm(ks[12], (DEPTH, D_MIX, D_MODEL), D_MIX ** -0.5)
    norm2_g = 1.0 + nrm(ks[13], (DEPTH, D_MODEL), 0.05)
    w_router = nrm(ks[14], (DEPTH, D_MODEL, N_EXPERTS), D_MODEL ** -0.5)
    b_router = nrm(ks[15], (DEPTH, N_EXPERTS), 0.01)
    w_gate = nrm(ks[16], (DEPTH, N_EXPERTS, D_MODEL, D_FF), D_MODEL ** -0.5)
    b_gate = nrm(ks[17], (DEPTH, N_EXPERTS, D_FF), 0.01)
    w_up = nrm(ks[18], (DEPTH, N_EXPERTS, D_MODEL, D_FF), D_MODEL ** -0.5)
    b_up = nrm(ks[19], (DEPTH, N_EXPERTS, D_FF), 0.01)
    w_down = nrm(ks[20], (DEPTH, N_EXPERTS, D_FF, D_MODEL), D_FF ** -0.5)
    b_down = nrm(ks[21], (DEPTH, N_EXPERTS, D_MODEL), 0.01)
    final_norm_g = 1.0 + nrm(ks[22], (D_MODEL,), 0.05)
    return {'x': x, 'c': c, 'positions': positions, 'w_ada': w_ada, 'b_ada': b_ada, 'norm1_g': norm1_g,
            'w_in': w_in, 'conv_w': conv_w, 'conv_b': conv_b, 'b_igate': b_igate, 'b_fgate': b_fgate,
            'mlstm_norm_g': mlstm_norm_g, 'w_out': w_out, 'norm2_g': norm2_g, 'w_router': w_router,
            'b_router': b_router, 'w_gate': w_gate, 'b_gate': b_gate, 'w_up': w_up, 'b_up': b_up,
            'w_down': w_down, 'b_down': b_down, 'final_norm_g': final_norm_g}


def reference(x, c, positions, w_ada, b_ada, norm1_g, w_in, conv_w, conv_b, b_igate, b_fgate,
              mlstm_norm_g, w_out, norm2_g, w_router, b_router, w_gate, b_gate, w_up, b_up,
              w_down, b_down, final_norm_g):
    cond = jax.nn.silu(c)
    for l in range(DEPTH):
        mod = cond @ w_ada[l] + b_ada[l]
        sh1, sc1, g1, sh2, sc2, g2 = [m[:, None, :] for m in jnp.split(mod, 6, axis=-1)]
        h = rms_norm(x, norm1_g[l]) * (1.0 + sc1) + sh1
        y = hybrid_mixer(h, positions, w_in[l], conv_w[l], conv_b[l], b_igate[l], b_fgate[l],
                         mlstm_norm_g[l], w_out[l])
        x = x + g1 * y
        h = rms_norm(x, norm2_g[l]) * (1.0 + sc2) + sh2
        y = moe_ffn(h, w_router[l], b_router[l], w_gate[l], b_gate[l], w_up[l], b_up[l], w_down[l], b_down[l])
        x = x + g2 * y
    return rms_norm(x, final_norm_g)
```

```python
import functools

import jax
import jax.numpy as jnp
from jax import lax
from jax.experimental import pallas as pl
from jax.experimental.pallas import tpu as pltpu

F32 = jnp.float32
BF16 = jnp.bfloat16
I32 = jnp.int32
U32 = jnp.uint32

M_HEADS = 4
M_HEAD_DIM = 128
M_WIDTH = M_HEADS * M_HEAD_DIM
A_HEADS = 8
A_HEAD_DIM = 64
A_WIDTH = A_HEADS * A_HEAD_DIM
CONV_K = 4
GATE_SOFTCAP = 15.0
MOBA_BLOCK = 256
MOBA_TOPK = 3
ROPE_THETA = 500000.0
ROPE_DIM = A_HEAD_DIM // 4
N_EXPERTS = 32
TOP_K = 4
SWIGLU_LIMIT = 7.0
SWIGLU_ALPHA = 1.702
NORM_EPS = 1e-5

LANES = 128
SUBLANES = 8
VMEM_LIMIT = 56 * 1024 * 1024

NEG_BIG = -1e30
MLSTM_L = 256
IN_TM = 256
OUT_TM = 512
MOE_TM = 512
EXPERT_R = 256

C_QK = 0
C_V = 2 * M_WIDTH
C_O = 3 * M_WIDTH
C_A = 4 * M_WIDTH
C_G = 4 * M_WIDTH + 3 * A_WIDTH
N_COLS = C_G + LANES


def _dot(a, b):
    return jnp.dot(a, b, preferred_element_type=F32)


def _dot_nt(a, b):
    return lax.dot_general(a, b, (((1,), (1,)), ((), ())), preferred_element_type=F32)


def _split(a):
    hi = a.astype(BF16)
    lo = (a - hi.astype(F32)).astype(BF16)
    return hi, lo


def _dot3(a, b):
    ah, al = _split(a)
    bh, bl = _split(b)
    return _dot(ah, bh) + (_dot(al, bh) + _dot(ah, bl))


def _sigmoid(t):
    return 1.0 / (1.0 + jnp.exp(-t))


def _params(sem):
    return pltpu.CompilerParams(dimension_semantics=sem, vmem_limit_bytes=VMEM_LIMIT)


def _pack_halves(t):
    w = t.shape[1] // 2
    lo = pltpu.bitcast(t[:, :w].astype(BF16).astype(F32), U32)
    hi = pltpu.bitcast(t[:, w:].astype(BF16).astype(F32), U32)
    return (lo >> 16) | (hi & jnp.uint32(0xFFFF0000))


def _unpack_lo(p):
    return pltpu.bitcast(p << 16, F32)


def _unpack_hi(p):
    return pltpu.bitcast(p & jnp.uint32(0xFFFF0000), F32)


def _mod_kernel(c_ref, w_ref, b_ref, o_ref):
    c = c_ref[...]
    cond = c * _sigmoid(c)
    o_ref[0] = _dot3(cond, w_ref[0]) + b_ref[0]


def _modulation(c, w_ada, b_ada):
    depth, d, n6 = w_ada.shape
    b = c.shape[0]
    rows = -(-b // SUBLANES) * SUBLANES
    cp = jnp.zeros((rows, d), F32).at[:b].set(c)
    tn = 1536
    out = pl.pallas_call(
        _mod_kernel,
        grid=(depth, n6 // tn),
        in_specs=[pl.BlockSpec((rows, d), lambda l, j: (0, 0)),
                  pl.BlockSpec((1, d, tn), lambda l, j: (l, 0, j)),
                  pl.BlockSpec((1, 1, tn), lambda l, j: (l, 0, j))],
        out_specs=pl.BlockSpec((1, rows, tn), lambda l, j: (l, 0, j)),
        out_shape=jax.ShapeDtypeStruct((depth, rows, n6), F32),
        compiler_params=_params(("arbitrary", "arbitrary")),
        name="adaln_mod",
    )(cp, w_ada, b_ada.reshape(depth, 1, n6))
    return out[:, :b].reshape(depth, b, 6, 1, d)


def _rope_kernel(pos_ref, invf_ref, sign_ref, cos_ref, sin_ref):
    ang = pos_ref[0].astype(F32) * invf_ref[...]
    cos_ref[0] = jnp.cos(ang)
    sin_ref[0] = jnp.sin(ang) * sign_ref[...]


def _rope_tables(positions):
    b, s = positions.shape
    half = ROPE_DIM // 2
    inv_freq = ROPE_THETA ** (-jnp.arange(0, ROPE_DIM, 2, dtype=F32) / ROPE_DIM)
    lane = jnp.arange(LANES) % A_HEAD_DIM
    invf = jnp.where(lane < ROPE_DIM, inv_freq[lane % half], 0.0).astype(F32).reshape(1, LANES)
    sign = jnp.where(lane < half, -1.0, jnp.where(lane < ROPE_DIM, 1.0, 0.0)).astype(F32).reshape(1, LANES)
    ts = min(s, 1024)
    tab = jax.ShapeDtypeStruct((b, s, LANES), F32)
    return pl.pallas_call(
        _rope_kernel,
        grid=(b, s // ts),
        in_specs=[pl.BlockSpec((1, ts, 1), lambda i, j: (i, j, 0)),
                  pl.BlockSpec((1, LANES), lambda i, j: (0, 0)),
                  pl.BlockSpec((1, LANES), lambda i, j: (0, 0))],
        out_specs=[pl.BlockSpec((1, ts, LANES), lambda i, j: (i, j, 0))] * 2,
        out_shape=[tab, tab],
        compiler_params=_params(("arbitrary", "arbitrary")),
        name="rope_tables",
    )(positions.reshape(b, s, 1), invf, sign)


def _inproj_kernel(x_ref, g_ref, sc_ref, sh_ref, w_ref, cw_ref, cb_ref, gb_ref, cos_ref, sin_ref,
                   zqk_ref, zv_ref, zo_ref, za_ref, zg_ref, km_ref, halo_ref):
    tm = x_ref.shape[1]
    x = x_ref[0]
    var = jnp.mean(x * x, axis=-1, keepdims=True)
    h = x * lax.rsqrt(var + NORM_EPS) * g_ref[...]
    h = h * (1.0 + sc_ref[0]) + sh_ref[0]
    hb = h.astype(BF16)

    @pl.when(pl.program_id(1) == 0)
    def _():
        halo_ref[...] = jnp.zeros_like(halo_ref)

    z = _dot(hb, w_ref[:, C_QK:C_V])
    prev = halo_ref[...]
    row8 = lax.broadcasted_iota(I32, prev.shape, 0)
    acc = z * cw_ref[CONV_K - 1:CONV_K, :] + cb_ref[...]
    for k in range(1, CONV_K):
        zs = pltpu.roll(z, k, 0)
        ps = pltpu.roll(prev, k, 0)
        head = jnp.where(row8 < k, ps, zs[:SUBLANES])
        zs = jnp.concatenate([head, zs[SUBLANES:]], axis=0)
        acc = acc + zs * cw_ref[CONV_K - 1 - k:CONV_K - k, :]
    halo_ref[...] = z[tm - SUBLANES:, :]
    qk = acc * _sigmoid(acc)
    col = lax.broadcasted_iota(I32, (1, 2 * M_WIDTH), 1)
    kscale = jnp.where(col < M_WIDTH, 1.0, M_HEAD_DIM ** -0.5).astype(F32)
    zqk_ref[0] = (qk * kscale).astype(BF16)

    zv_ref[0] = _dot(hb, w_ref[:, C_V:C_O]).astype(BF16)
    zo_ref[0] = _dot(hb, w_ref[:, C_O:C_A]).astype(BF16)

    za = _dot(hb, w_ref[:, C_A:C_G])
    cos = cos_ref[0]
    sin = sin_ref[0]
    lane = lax.broadcasted_iota(I32, (tm, LANES), 1)
    first_half = (lane % A_HEAD_DIM) < (ROPE_DIM // 2)
    for c in range(2 * A_WIDTH // LANES):
        xc = za[:, c * LANES:(c + 1) * LANES]
        partner = jnp.where(first_half, pltpu.roll(xc, LANES - ROPE_DIM // 2, 1), pltpu.roll(xc, ROPE_DIM // 2, 1))
        rc = xc * cos + partner * sin
        za_ref[0, :, c * LANES:(c + 1) * LANES] = rc.astype(BF16)
        if c >= A_WIDTH // LANES:
            kc = c - A_WIDTH // LANES
            km_ref[0, 0, :, kc * LANES:(kc + 1) * LANES] = jnp.mean(rc, axis=0, keepdims=True)
    za_ref[0, :, 2 * A_WIDTH:] = za[:, 2 * A_WIDTH:].astype(BF16)

    t = _dot(hb, w_ref[:, C_G:N_COLS]) + gb_ref[...]
    t = GATE_SOFTCAP * jnp.tanh(t / GATE_SOFTCAP)
    logsig = jnp.minimum(t, 0.0) - jnp.log(1.0 + jnp.exp(-jnp.abs(t)))
    zg_ref[0] = jnp.where(lane < M_HEADS, t, logsig)


def _in_projection(x, g, sc, sh, w_perm, conv_w, conv_b, gate_b, cos_t, sin_t):
    b, s, d = x.shape
    tm = IN_TM
    assert tm == MOBA_BLOCK and s % tm == 0
    nt = s // tm
    row = lambda i, j: (i, j, 0)
    bcast = lambda i, j: (i, 0, 0)
    const = lambda i, j: (0, 0)
    out_shape = [
        jax.ShapeDtypeStruct((b, s, 2 * M_WIDTH), BF16),
        jax.ShapeDtypeStruct((b, s, M_WIDTH), BF16),
        jax.ShapeDtypeStruct((b, s, M_WIDTH), BF16),
        jax.ShapeDtypeStruct((b, s, 3 * A_WIDTH), BF16),
        jax.ShapeDtypeStruct((b, s, LANES), F32),
        jax.ShapeDtypeStruct((b, nt, 1, A_WIDTH), F32),
    ]
    out_specs = [
        pl.BlockSpec((1, tm, 2 * M_WIDTH), row),
        pl.BlockSpec((1, tm, M_WIDTH), row),
        pl.BlockSpec((1, tm, M_WIDTH), row),
        pl.BlockSpec((1, tm, 3 * A_WIDTH), row),
        pl.BlockSpec((1, tm, LANES), row),
        pl.BlockSpec((1, 1, 1, A_WIDTH), lambda i, j: (i, j, 0, 0)),
    ]
    in_specs = [
        pl.BlockSpec((1, tm, d), row),
        pl.BlockSpec((1, d), const),
        pl.BlockSpec((1, 1, d), bcast),
        pl.BlockSpec((1, 1, d), bcast),
        pl.BlockSpec((d, N_COLS), const),
        pl.BlockSpec((CONV_K, 2 * M_WIDTH), const),
        pl.BlockSpec((1, 2 * M_WIDTH), const),
        pl.BlockSpec((1, LANES), const),
        pl.BlockSpec((1, tm, LANES), row),
        pl.BlockSpec((1, tm, LANES), row),
    ]
    return pl.pallas_call(
        _inproj_kernel,
        grid=(b, nt),
        in_specs=in_specs,
        out_specs=out_specs,
        out_shape=out_shape,
        scratch_shapes=[pltpu.VMEM((SUBLANES, 2 * M_WIDTH), F32)],
        compiler_params=_params(("arbitrary", "arbitrary")),
        name="in_proj",
    )(x, g, sc, sh, w_perm, conv_w, conv_b, gate_b, cos_t, sin_t)


def _mlstm_kernel(qk_ref, v_ref, o_ref, g_ref, ng_ref, out_ref, c_scr, n_scr, m_scr):
    L = qk_ref.shape[1]

    @pl.when(pl.program_id(1) == 0)
    def _():
        c_scr[...] = jnp.zeros_like(c_scr)
        n_scr[...] = jnp.zeros_like(n_scr)
        m_scr[...] = jnp.full_like(m_scr, -jnp.inf)

    gates = g_ref[0]
    r = lax.broadcasted_iota(I32, (L, L), 0)
    c = lax.broadcasted_iota(I32, (L, L), 1)
    causal = c <= r
    tri = causal.astype(BF16)
    gh, gl = _split(gates)
    cum = _dot(tri, gh) + _dot(tri, gl)
    lane = lax.broadcasted_iota(I32, gates.shape, 1)
    mix = jnp.where(lane < M_HEADS, gates, cum)
    mix_t = mix.T

    for hd in range(M_HEADS):
        q = qk_ref[0, :, hd * M_HEAD_DIM:(hd + 1) * M_HEAD_DIM]
        k = qk_ref[0, :, M_WIDTH + hd * M_HEAD_DIM:M_WIDTH + (hd + 1) * M_HEAD_DIM]
        v = v_ref[0, :, hd * M_HEAD_DIM:(hd + 1) * M_HEAD_DIM]
        ig_col = mix[:, hd:hd + 1]
        b_col = mix[:, M_HEADS + hd:M_HEADS + hd + 1]
        ig_row = mix_t[hd:hd + 1, :]
        b_row = mix_t[M_HEADS + hd:M_HEADS + hd + 1, :]
        g_tot = b_row[:, L - 1:L]
        m_prev = m_scr[hd, 0:1, 0:1]
        c_prev = c_scr[hd]
        n_prev = n_scr[hd, 0:1, :]

        a_col = b_col + m_prev
        dmat = jnp.where(causal, b_col - b_row + ig_row, -jnp.inf)
        m_t = jnp.maximum(a_col, jnp.max(dmat, axis=1, keepdims=True))
        s = _dot_nt(q, k) * jnp.exp(dmat - m_t)
        inter = jnp.exp(a_col - m_t)
        num = _dot(s.astype(BF16), v) + inter * _dot(q, c_prev.astype(BF16))
        qn = jnp.sum(q.astype(F32) * n_prev, axis=1, keepdims=True)
        den = jnp.sum(s, axis=1, keepdims=True) + inter * qn
        hh = num / jnp.maximum(jnp.abs(den), jnp.exp(-m_t))

        w_row = g_tot - b_row + ig_row
        m_new = jnp.maximum(g_tot + m_prev, jnp.max(w_row, axis=1, keepdims=True))
        decay = jnp.exp(g_tot + m_prev - m_new)
        kw = k.astype(F32) * jnp.exp(g_tot - b_col + ig_col - m_new)
        c_scr[hd] = decay * c_prev + _dot(kw.T.astype(BF16), v)
        n_scr[hd] = jnp.broadcast_to(decay * n_prev + jnp.sum(kw, axis=0, keepdims=True), n_scr.shape[1:])
        m_scr[hd] = jnp.broadcast_to(m_new, m_scr.shape[1:])

        og = _sigmoid(o_ref[0, :, hd * M_HEAD_DIM:(hd + 1) * M_HEAD_DIM].astype(F32))
        ho = hh * og
        hn = ho * lax.rsqrt(jnp.mean(ho * ho, axis=-1, keepdims=True) + NORM_EPS)
        out_ref[0, :, hd * M_HEAD_DIM:(hd + 1) * M_HEAD_DIM] = (
            hn * ng_ref[:, hd * M_HEAD_DIM:(hd + 1) * M_HEAD_DIM]).astype(BF16)


def _mlstm(zqk, zv, zo, zg, norm_g):
    b, s, _ = zqk.shape
    L = min(MLSTM_L, s)
    row = lambda i, j: (i, j, 0)
    return pl.pallas_call(
        _mlstm_kernel,
        grid=(b, s // L),
        in_specs=[pl.BlockSpec((1, L, 2 * M_WIDTH), row),
                  pl.BlockSpec((1, L, M_WIDTH), row),
                  pl.BlockSpec((1, L, M_WIDTH), row),
                  pl.BlockSpec((1, L, LANES), row),
                  pl.BlockSpec((1, M_WIDTH), lambda i, j: (0, 0))],
        out_specs=pl.BlockSpec((1, L, M_WIDTH), row),
        out_shape=jax.ShapeDtypeStruct((b, s, M_WIDTH), BF16),
        scratch_shapes=[pltpu.VMEM((M_HEADS, M_HEAD_DIM, M_HEAD_DIM), F32),
                        pltpu.VMEM((M_HEADS, SUBLANES, M_HEAD_DIM), F32),
                        pltpu.VMEM((M_HEADS, SUBLANES, LANES), F32)],
        compiler_params=_params(("arbitrary", "arbitrary")),
        name="mlstm",
    )(zqk, zv, zo, zg, norm_g)


def _moba_kernel(q_ref, k_ref, v_ref, km_ref, o_ref):
    tq = q_ref.shape[1]
    nb = km_ref.shape[1]
    own = pl.program_id(2)

    q = q_ref[0]
    lane = lax.broadcasted_iota(I32, q.shape, 1)
    head0 = lane < A_HEAD_DIM
    zero = jnp.zeros_like(q)
    q2 = jnp.concatenate([jnp.where(head0, q, zero), jnp.where(head0, zero, q)], axis=0)
    qs2 = (q2.astype(F32) * (A_HEAD_DIM ** -0.5)).astype(BF16)

    kmh, kml = _split(km_ref[0])
    gate = _dot_nt(q2, kmh) + _dot_nt(q2, kml)
    blk = lax.broadcasted_iota(I32, gate.shape, 1)
    g0 = jnp.where(blk < own, gate, -jnp.inf)
    sel = jnp.zeros(gate.shape, jnp.bool_)
    for _ in range(MOBA_TOPK):
        mx = jnp.max(g0, axis=1, keepdims=True)
        hit = jnp.logical_and(g0 == mx, mx > -jnp.inf)
        idx = jnp.min(jnp.where(hit, blk, nb), axis=1, keepdims=True)
        pick = blk == idx
        sel = jnp.logical_or(sel, pick)
        g0 = jnp.where(pick, -jnp.inf, g0)
    bias = jnp.where(sel, 0.0, NEG_BIG).astype(F32)

    off = pl.multiple_of(own * tq, tq)
    kj = k_ref[0, pl.ds(off, tq), :]
    vj = v_ref[0, pl.ds(off, tq), :]
    r = lax.broadcasted_iota(I32, (2 * tq, tq), 0)
    c = lax.broadcasted_iota(I32, (2 * tq, tq), 1)
    s = jnp.where(c <= (r % tq), _dot_nt(qs2, kj), NEG_BIG)
    m = jnp.max(s, axis=1, keepdims=True)
    p = jnp.exp(s - m)
    l = jnp.sum(p, axis=1, keepdims=True)
    acc = _dot(p.astype(BF16), vj)

    def body(j, carry):
        m, l, acc = carry
        offj = pl.multiple_of(j * tq, tq)
        kj = k_ref[0, pl.ds(offj, tq), :]
        vj = v_ref[0, pl.ds(offj, tq), :]
        bcol = jnp.sum(jnp.where(blk == j, bias, 0.0), axis=1, keepdims=True)
        s = _dot_nt(qs2, kj) + bcol
        mn = jnp.maximum(m, jnp.max(s, axis=1, keepdims=True))
        alpha = jnp.exp(m - mn)
        p = jnp.exp(s - mn)
        l = alpha * l + jnp.sum(p, axis=1, keepdims=True)
        acc = alpha * acc + _dot(p.astype(BF16), vj)
        return mn, l, acc

    m, l, acc = lax.fori_loop(0, own, body, (m, l, acc))
    out = acc / l
    o_ref[0] = jnp.where(head0, out[:tq], out[tq:]).astype(BF16)


def _moba(za, kmean):
    b, s, _ = za.shape
    tq = MOBA_BLOCK
    nb = s // tq
    pairs = A_WIDTH // LANES
    return pl.pallas_call(
        _moba_kernel,
        grid=(b, pairs, nb),
        in_specs=[pl.BlockSpec((1, tq, LANES), lambda i, h, j: (i, j, h)),
                  pl.BlockSpec((1, s, LANES), lambda i, h, j: (i, 0, pairs + h)),
                  pl.BlockSpec((1, s, LANES), lambda i, h, j: (i, 0, 2 * pairs + h)),
                  pl.BlockSpec((1, nb, LANES), lambda i, h, j: (i, 0, h))],
        out_specs=pl.BlockSpec((1, tq, LANES), lambda i, h, j: (i, j, h)),
        out_shape=jax.ShapeDtypeStruct((b, s, A_WIDTH), BF16),
        compiler_params=_params(("arbitrary", "arbitrary", "arbitrary")),
        name="moba",
    )(za, za, za, kmean)


def _outproj_kernel(hm_ref, ha_ref, wo_ref, x_ref, g1_ref, n2_ref, sc_ref, sh_ref, wrh_ref, wrl_ref, br_ref,
                    x1_ref, h2p_ref, ri_ref, rp_ref, cnt_ref, cnt_scr):
    tm = x_ref.shape[1]

    @pl.when(jnp.logical_and(pl.program_id(0) == 0, pl.program_id(1) == 0))
    def _():
        cnt_scr[...] = jnp.zeros_like(cnt_scr)

    y = _dot(hm_ref[0], wo_ref[:M_WIDTH, :]) + _dot(ha_ref[0], wo_ref[M_WIDTH:, :])
    x1 = x_ref[0] + g1_ref[0] * y
    x1_ref[0] = x1
    var = jnp.mean(x1 * x1, axis=-1, keepdims=True)
    h2 = x1 * lax.rsqrt(var + NORM_EPS) * n2_ref[...]
    h2 = h2 * (1.0 + sc_ref[0]) + sh_ref[0]
    h2p_ref[0] = _pack_halves(h2)

    hh, hl = _split(h2)
    logits = _dot(hh, wrh_ref[...]) + (_dot(hl, wrh_ref[...]) + _dot(hh, wrl_ref[...])) + br_ref[...]
    lane = lax.broadcasted_iota(I32, logits.shape, 1)
    g0 = logits
    vals, idxs = [], []
    for _ in range(TOP_K):
        mx = jnp.max(g0, axis=1, keepdims=True)
        idx = jnp.min(jnp.where(g0 == mx, lane, LANES), axis=1, keepdims=True)
        vals.append(mx)
        idxs.append(idx)
        g0 = jnp.where(lane == idx, -jnp.inf, g0)
    ex = [jnp.exp(vv - vals[0]) for vv in vals]
    tot = ex[0] + ex[1] + ex[2] + ex[3]

    onehot = jnp.zeros(logits.shape, F32)
    for idx in idxs:
        onehot = onehot + (lane == idx).astype(F32)
    r = lax.broadcasted_iota(I32, (tm, tm), 0)
    c = lax.broadcasted_iota(I32, (tm, tm), 1)
    before = _dot((c < r).astype(BF16), onehot.astype(BF16)) + cnt_scr[0:1, :]
    ri = jnp.zeros(logits.shape, I32)
    rp = jnp.zeros(logits.shape, F32)
    for kk in range(TOP_K):
        rank = jnp.sum(jnp.where(lane == idxs[kk], before, 0.0), axis=1, keepdims=True)
        ri = jnp.where(lane == kk, idxs[kk], ri)
        ri = jnp.where(lane == TOP_K + kk, rank.astype(I32), ri)
        rp = jnp.where(lane == kk, ex[kk] / tot, rp)
    ri_ref[0] = ri
    rp_ref[0] = rp
    cnt = cnt_scr[...] + jnp.sum(onehot, axis=0, keepdims=True)
    cnt_scr[...] = cnt
    cnt_ref[...] = cnt


def _out_projection(hm, ha, w_out, x, g1, n2g, sc2, sh2, wr_hi, wr_lo, b_r):
    b, s, d = x.shape
    tm = min(OUT_TM, s)
    row = lambda i, j: (i, j, 0)
    bcast = lambda i, j: (i, 0, 0)
    const = lambda i, j: (0, 0)
    out_shape = [
        jax.ShapeDtypeStruct((b, s, d), F32),
        jax.ShapeDtypeStruct((b, s, d // 2), U32),
        jax.ShapeDtypeStruct((b, s, LANES), I32),
        jax.ShapeDtypeStruct((b, s, LANES), F32),
        jax.ShapeDtypeStruct((SUBLANES, LANES), F32),
    ]
    out_specs = [
        pl.BlockSpec((1, tm, d), row),
        pl.BlockSpec((1, tm, d // 2), row),
        pl.BlockSpec((1, tm, LANES), row),
        pl.BlockSpec((1, tm, LANES), row),
        pl.BlockSpec((SUBLANES, LANES), const),
    ]
    in_specs = [
        pl.BlockSpec((1, tm, M_WIDTH), row),
        pl.BlockSpec((1, tm, A_WIDTH), row),
        pl.BlockSpec((M_WIDTH + A_WIDTH, d), const),
        pl.BlockSpec((1, tm, d), row),
        pl.BlockSpec((1, 1, d), bcast),
        pl.BlockSpec((1, d), const),
        pl.BlockSpec((1, 1, d), bcast),
        pl.BlockSpec((1, 1, d), bcast),
        pl.BlockSpec((d, LANES), const),
        pl.BlockSpec((d, LANES), const),
        pl.BlockSpec((1, LANES), const),
    ]
    return pl.pallas_call(
        _outproj_kernel,
        grid=(b, s // tm),
        in_specs=in_specs,
        out_specs=out_specs,
        out_shape=out_shape,
        scratch_shapes=[pltpu.VMEM((SUBLANES, LANES), F32)],
        compiler_params=_params(("arbitrary", "arbitrary")),
        name="out_proj_router",
    )(hm, ha, w_out, x, g1, n2g, sc2, sh2, wr_hi, wr_lo, b_r)


def _row_copy_wait(src, dst, sem):
    pltpu.make_async_copy(src, dst, sem).wait()


def _scatter_kernel(slot_hbm, h2p_ref, xs_in, xs_out, idx_smem, isem, sem):
    del xs_in
    tm = h2p_ref.shape[0]
    i = pl.program_id(0)
    cp = pltpu.make_async_copy(slot_hbm.at[i], idx_smem, isem)
    cp.start()
    cp.wait()

    def body(r, carry):
        for kk in range(TOP_K):
            dst = idx_smem[r * TOP_K + kk]
            pltpu.make_async_copy(h2p_ref.at[pl.ds(r, 1)], xs_out.at[pl.ds(dst, 1)], sem).start()
        return carry

    lax.fori_loop(0, tm, body, 0)
    for kk in range(TOP_K):
        _row_copy_wait(h2p_ref, xs_out.at[pl.ds(0, tm)], sem)


def _scatter_rows(slot_tiles, h2p, n_slots):
    t, w = h2p.shape
    tm = slot_tiles.shape[1] // TOP_K
    xs0 = jnp.zeros((n_slots, w), U32)
    return pl.pallas_call(
        _scatter_kernel,
        grid=(t // tm,),
        in_specs=[pl.BlockSpec(memory_space=pl.ANY),
                  pl.BlockSpec((tm, w), lambda i: (i, 0)),
                  pl.BlockSpec(memory_space=pl.ANY)],
        out_specs=pl.BlockSpec(memory_space=pl.ANY),
        out_shape=jax.ShapeDtypeStruct((n_slots, w), U32),
        scratch_shapes=[pltpu.SMEM((tm * TOP_K,), I32),
                        pltpu.SemaphoreType.DMA(()),
                        pltpu.SemaphoreType.DMA(())],
        input_output_aliases={2: 0},
        compiler_params=_params(("arbitrary",)),
        name="moe_scatter",
    )(slot_tiles, h2p, xs0)


def _expert_kernel(be_ref, bi_ref, nb_ref, xs_ref, wg_ref, bg_ref, wu_ref, bu_ref, wd_ref, bd_ref, yb_ref):
    del be_ref, bi_ref
    half = xs_ref.shape[1]

    @pl.when(pl.program_id(0) < nb_ref[0])
    def _():
        xp = xs_ref[...]
        xlo = _unpack_lo(xp).astype(BF16)
        xhi = _unpack_hi(xp).astype(BF16)
        gt = _dot(xlo, wg_ref[0, :half, :]) + _dot(xhi, wg_ref[0, half:, :]) + bg_ref[0]
        up = _dot(xlo, wu_ref[0, :half, :]) + _dot(xhi, wu_ref[0, half:, :]) + bu_ref[0]
        gt = jnp.minimum(gt, SWIGLU_LIMIT)
        up = jnp.clip(up, -SWIGLU_LIMIT, SWIGLU_LIMIT)
        act = (up + 1.0) * (gt * _sigmoid(SWIGLU_ALPHA * gt))
        y = _dot(act.astype(BF16), wd_ref[0]) + bd_ref[0]
        yb_ref[...] = _pack_halves(y)

    @pl.when(pl.program_id(0) >= nb_ref[0])
    def _():
        yb_ref[...] = jnp.zeros_like(yb_ref)


def _experts(block_e, block_i, n_used, xs, wg, bg, wu, bu, wd, bd):
    n_slots, half = xs.shape
    r = EXPERT_R
    ne, d, dff = wg.shape
    wmap = lambda i, be, bi, nb: (be[i], 0, 0)
    xmap = lambda i, be, bi, nb: (bi[i], 0)
    grid_spec = pltpu.PrefetchScalarGridSpec(
        num_scalar_prefetch=3,
        grid=(n_slots // r,),
        in_specs=[pl.BlockSpec((r, half), xmap),
                  pl.BlockSpec((1, d, dff), wmap),
                  pl.BlockSpec((1, 1, dff), wmap),
                  pl.BlockSpec((1, d, dff), wmap),
                  pl.BlockSpec((1, 1, dff), wmap),
                  pl.BlockSpec((1, dff, d), wmap),
                  pl.BlockSpec((1, 1, d), wmap)],
        out_specs=pl.BlockSpec((r, half), lambda i, be, bi, nb: (i, 0)),
    )
    return pl.pallas_call(
        _expert_kernel,
        grid_spec=grid_spec,
        out_shape=jax.ShapeDtypeStruct((n_slots, half), U32),
        compiler_params=_params(("arbitrary",)),
        name="moe_experts",
    )(block_e, block_i, n_used, xs, wg, bg.reshape(ne, 1, dff), wu, bu.reshape(ne, 1, dff), wd, bd.reshape(ne, 1, d))


def _combine_kernel(slot_hbm, yb_hbm, rp_ref, x1_ref, g2_ref, fg_ref, out_ref, buf, idx_smem, isem, sem, *, final):
    tm = x1_ref.shape[1]
    half = buf.shape[2]
    i = pl.program_id(0) * pl.num_programs(1) + pl.program_id(1)
    cp = pltpu.make_async_copy(slot_hbm.at[i], idx_smem, isem)
    cp.start()
    cp.wait()

    def body(r, carry):
        for kk in range(TOP_K):
            src = idx_smem[r * TOP_K + kk]
            pltpu.make_async_copy(yb_hbm.at[pl.ds(src, 1)], buf.at[kk, pl.ds(r, 1)], sem).start()
        return carry

    lax.fori_loop(0, tm, body, 0)
    for kk in range(TOP_K):
        _row_copy_wait(yb_hbm.at[pl.ds(0, tm)], buf.at[kk], sem)

    rp = rp_ref[0]
    ylo = jnp.zeros((tm, half), F32)
    yhi = jnp.zeros((tm, half), F32)
    for kk in range(TOP_K):
        w = buf[kk]
        pk = rp[:, kk:kk + 1]
        ylo = ylo + pk * _unpack_lo(w)
        yhi = yhi + pk * _unpack_hi(w)
    x2 = x1_ref[0] + g2_ref[0] * jnp.concatenate([ylo, yhi], axis=1)
    if final:
        x2 = x2 * lax.rsqrt(jnp.mean(x2 * x2, axis=-1, keepdims=True) + NORM_EPS) * fg_ref[...]
    out_ref[0] = x2


def _combine(slot_tiles, yb, rp, x1, g2, fg, final):
    b, s, d = x1.shape
    tm = slot_tiles.shape[1] // TOP_K
    row = lambda i, j: (i, j, 0)
    return pl.pallas_call(
        functools.partial(_combine_kernel, final=final),
        grid=(b, s // tm),
        in_specs=[pl.BlockSpec(memory_space=pl.ANY),
                  pl.BlockSpec(memory_space=pl.ANY),
                  pl.BlockSpec((1, tm, LANES), row),
                  pl.BlockSpec((1, tm, d), row),
                  pl.BlockSpec((1, 1, d), lambda i, j: (i, 0, 0)),
                  pl.BlockSpec((1, d), lambda i, j: (0, 0))],
        out_specs=pl.BlockSpec((1, tm, d), row),
        out_shape=jax.ShapeDtypeStruct((b, s, d), F32),
        scratch_shapes=[pltpu.VMEM((TOP_K, tm, d // 2), U32),
                        pltpu.SMEM((tm * TOP_K,), I32),
                        pltpu.SemaphoreType.DMA(()),
                        pltpu.SemaphoreType.DMA(())],
        compiler_params=_params(("arbitrary", "arbitrary")),
        name="moe_combine",
    )(slot_tiles, yb, rp, x1, g2, fg)


def _permute_w_in(w_in):
    d = w_in.shape[0]
    g0 = 4 * M_WIDTH
    a0 = g0 + 2 * M_HEADS
    pad = jnp.zeros((d, LANES - 2 * M_HEADS), w_in.dtype)
    return jnp.concatenate([w_in[:, :g0], w_in[:, a0:], w_in[:, g0:a0], pad], axis=1).astype(BF16)


def kernel(x, c, positions, w_ada, b_ada, norm1_g, w_in, conv_w, conv_b, b_igate, b_fgate, mlstm_norm_g,
           w_out, norm2_g, w_router, b_router, w_gate, b_gate, w_up, b_up, w_down, b_down, final_norm_g):
    b, s, d = x.shape
    depth = w_ada.shape[0]
    t = b * s
    tm = min(MOE_TM, s)
    n_slots = (t * TOP_K // EXPERT_R + N_EXPERTS) * EXPERT_R
    n_blocks = n_slots // EXPERT_R

    mod = _modulation(c, w_ada, b_ada)
    cos_t, sin_t = _rope_tables(positions)

    for l in range(depth):
        sh1, sc1, g1, sh2, sc2, g2 = [mod[l, :, i] for i in range(6)]
        gate_b = jnp.zeros((1, LANES), F32).at[0, :M_HEADS].set(b_igate[l]).at[0, M_HEADS:2 * M_HEADS].set(b_fgate[l])
        zqk, zv, zo, za, zg, kmean = _in_projection(
            x, norm1_g[l].reshape(1, d), sc1, sh1, _permute_w_in(w_in[l]), conv_w[l],
            conv_b[l].reshape(1, -1), gate_b, cos_t, sin_t)
        hm = _mlstm(zqk, zv, zo, zg, mlstm_norm_g[l].reshape(1, -1))
        ha = _moba(za, kmean.reshape(b, s // MOBA_BLOCK, A_WIDTH))

        wr = jnp.zeros((d, LANES), F32).at[:, :N_EXPERTS].set(w_router[l])
        wr_hi = wr.astype(BF16)
        wr_lo = (wr - wr_hi.astype(F32)).astype(BF16)
        b_r = jnp.full((1, LANES), NEG_BIG, F32).at[0, :N_EXPERTS].set(b_router[l])
        x1, h2p, ri, rp, cnt = _out_projection(
            hm, ha, w_out[l].astype(BF16), x, g1, norm2_g[l].reshape(1, d), sc2, sh2, wr_hi, wr_lo, b_r)

        counts = cnt[0, :N_EXPERTS].astype(I32)
        nblk = (counts + EXPERT_R - 1) // EXPERT_R
        blk_end = jnp.cumsum(nblk)
        blk_start = blk_end - nblk
        n_used = blk_end[-1:]
        block_i = jnp.minimum(jnp.arange(n_blocks, dtype=I32), n_used[0] - 1)
        block_e = jnp.minimum(jnp.searchsorted(blk_end, block_i, side='right'), N_EXPERTS - 1).astype(I32)
        eid = ri[..., :TOP_K].reshape(t, TOP_K)
        rank = ri[..., TOP_K:2 * TOP_K].reshape(t, TOP_K)
        slot = blk_start[eid] * EXPERT_R + rank
        slot_tiles = slot.reshape(t // tm, tm * TOP_K)

        xs = _scatter_rows(slot_tiles, h2p.reshape(t, d // 2), n_slots)
        yb = _experts(block_e, block_i, n_used.astype(I32), xs,
                      w_gate[l].astype(BF16), b_gate[l], w_up[l].astype(BF16), b_up[l],
                      w_down[l].astype(BF16), b_down[l])
        x = _combine(slot_tiles, yb, rp, x1, g2, final_norm_g.reshape(1, d), final=(l == depth - 1))
    return x
```

```python
import functools

import jax
import jax.numpy as jnp
from jax import lax
from jax.experimental import pallas as pl
from jax.experimental.pallas import tpu as pltpu

F32 = jnp.float32
BF16 = jnp.bfloat16
I32 = jnp.int32
U32 = jnp.uint32

M_HEADS = 4
M_HEAD_DIM = 128
M_WIDTH = M_HEADS * M_HEAD_DIM
A_HEADS = 8
A_HEAD_DIM = 64
A_WIDTH = A_HEADS * A_HEAD_DIM
CONV_K = 4
GATE_SOFTCAP = 15.0
MOBA_BLOCK = 256
MOBA_TOPK = 3
ROPE_THETA = 500000.0
ROPE_DIM = A_HEAD_DIM // 4
N_EXPERTS = 32
TOP_K = 4
SWIGLU_LIMIT = 7.0
SWIGLU_ALPHA = 1.702
NORM_EPS = 1e-5

LANES = 128
SUBLANES = 8
VMEM_LIMIT = 56 * 1024 * 1024

NEG_BIG = -1e30
MLSTM_L = 256
IN_TM = 256
OUT_TM = 512
MOE_TM = 512
EXPERT_R = 256

C_QK = 0
C_V = 2 * M_WIDTH
C_O = 3 * M_WIDTH
C_A = 4 * M_WIDTH
C_G = 4 * M_WIDTH + 3 * A_WIDTH
N_COLS = C_G + LANES


def _dot(a, b):
    return jnp.dot(a, b, preferred_element_type=F32)


def _dot_nt(a, b):
    return lax.dot_general(a, b, (((1,), (1,)), ((), ())), preferred_element_type=F32)


def _split(a):
    hi = a.astype(BF16)
    lo = (a - hi.astype(F32)).astype(BF16)
    return hi, lo


def _dot3(a, b):
    ah, al = _split(a)
    bh, bl = _split(b)
    return _dot(ah, bh) + (_dot(al, bh) + _dot(ah, bl))


def _sigmoid(t):
    return 1.0 / (1.0 + jnp.exp(-t))


def _params(sem):
    return pltpu.CompilerParams(dimension_semantics=sem, vmem_limit_bytes=VMEM_LIMIT)


def _pack_halves(t):
    w = t.shape[1] // 2
    lo = pltpu.bitcast(t[:, :w].astype(BF16).astype(F32), U32)
    hi = pltpu.bitcast(t[:, w:].astype(BF16).astype(F32), U32)
    return (lo >> 16) | (hi & jnp.uint32(0xFFFF0000))


def _unpack_lo(p):
    return pltpu.bitcast(p << 16, F32)


def _unpack_hi(p):
    return pltpu.bitcast(p & jnp.uint32(0xFFFF0000), F32)


def _mod_kernel(c_ref, w_ref, b_ref, o_ref):
    c = c_ref[...]
    cond = c * _sigmoid(c)
    o_ref[0] = _dot3(cond, w_ref[0]) + b_ref[0]


def _modulation(c, w_ada, b_ada):
    depth, d, n6 = w_ada.shape
    b = c.shape[0]
    rows = -(-b // SUBLANES) * SUBLANES
    cp = jnp.zeros((rows, d), F32).at[:b].set(c)
    tn = 1536
    out = pl.pallas_call(
        _mod_kernel,
        grid=(depth, n6 // tn),
        in_specs=[pl.BlockSpec((rows, d), lambda l, j: (0, 0)),
                  pl.BlockSpec((1, d, tn), lambda l, j: (l, 0, j)),
                  pl.BlockSpec((1, 1, tn), lambda l, j: (l, 0, j))],
        out_specs=pl.BlockSpec((1, rows, tn), lambda l, j: (l, 0, j)),
        out_shape=jax.ShapeDtypeStruct((depth, rows, n6), F32),
        compiler_params=_params(("arbitrary", "arbitrary")),
        name="adaln_mod",
    )(cp, w_ada, b_ada.reshape(depth, 1, n6))
    return out[:, :b].reshape(depth, b, 6, 1, d)


def _rope_kernel(pos_ref, invf_ref, sign_ref, cos_ref, sin_ref):
    ang = pos_ref[0].astype(F32) * invf_ref[...]
    cos_ref[0] = jnp.cos(ang)
    sin_ref[0] = jnp.sin(ang) * sign_ref[...]


def _rope_tables(positions):
    b, s = positions.shape
    half = ROPE_DIM // 2
    inv_freq = ROPE_THETA ** (-jnp.arange(0, ROPE_DIM, 2, dtype=F32) / ROPE_DIM)
    lane = jnp.arange(LANES) % A_HEAD_DIM
    invf = jnp.where(lane < ROPE_DIM, inv_freq[lane % half], 0.0).astype(F32).reshape(1, LANES)
    sign = jnp.where(lane < half, -1.0, jnp.where(lane < ROPE_DIM, 1.0, 0.0)).astype(F32).reshape(1, LANES)
    ts = min(s, 1024)
    tab = jax.ShapeDtypeStruct((b, s, LANES), F32)
    return pl.pallas_call(
        _rope_kernel,
        grid=(b, s // ts),
        in_specs=[pl.BlockSpec((1, ts, 1), lambda i, j: (i, j, 0)),
                  pl.BlockSpec((1, LANES), lambda i, j: (0, 0)),
                  pl.BlockSpec((1, LANES), lambda i, j: (0, 0))],
        out_specs=[pl.BlockSpec((1, ts, LANES), lambda i, j: (i, j, 0))] * 2,
        out_shape=[tab, tab],
        compiler_params=_params(("arbitrary", "arbitrary")),
        name="rope_tables",
    )(positions.reshape(b, s, 1), invf, sign)


def _inproj_kernel(x_ref, g_ref, sc_ref, sh_ref, w_ref, cw_ref, cb_ref, gb_ref, cos_ref, sin_ref,
                   zqk_ref, zv_ref, zo_ref, za_ref, vt_ref, zg_ref, km_ref, halo_ref):
    tm = x_ref.shape[1]
    x = x_ref[0]
    var = jnp.mean(x * x, axis=-1, keepdims=True)
    h = x * lax.rsqrt(var + NORM_EPS) * g_ref[...]
    h = h * (1.0 + sc_ref[0]) + sh_ref[0]
    hb = h.astype(BF16)

    @pl.when(pl.program_id(1) == 0)
    def _():
        halo_ref[...] = jnp.zeros_like(halo_ref)

    z = _dot(hb, w_ref[:, C_QK:C_V])
    prev = halo_ref[...]
    row8 = lax.broadcasted_iota(I32, prev.shape, 0)
    acc = z * cw_ref[CONV_K - 1:CONV_K, :] + cb_ref[...]
    for k in range(1, CONV_K):
        zs = pltpu.roll(z, k, 0)
        ps = pltpu.roll(prev, k, 0)
        head = jnp.where(row8 < k, ps, zs[:SUBLANES])
        zs = jnp.concatenate([head, zs[SUBLANES:]], axis=0)
        acc = acc + zs * cw_ref[CONV_K - 1 - k:CONV_K - k, :]
    halo_ref[...] = z[tm - SUBLANES:, :]
    qk = acc * _sigmoid(acc)
    col = lax.broadcasted_iota(I32, (1, 2 * M_WIDTH), 1)
    kscale = jnp.where(col < M_WIDTH, 1.0, M_HEAD_DIM ** -0.5).astype(F32)
    zqk_ref[0] = (qk * kscale).astype(BF16)

    zv_ref[0] = _dot(hb, w_ref[:, C_V:C_O]).astype(BF16)
    zo_ref[0] = _dot(hb, w_ref[:, C_O:C_A]).astype(BF16)

    za = _dot(hb, w_ref[:, C_A:C_G])
    cos = cos_ref[0]
    sin = sin_ref[0]
    lane = lax.broadcasted_iota(I32, (tm, LANES), 1)
    first_half = (lane % A_HEAD_DIM) < (ROPE_DIM // 2)
    for c in range(2 * A_WIDTH // LANES):
        xc = za[:, c * LANES:(c + 1) * LANES]
        partner = jnp.where(first_half, pltpu.roll(xc, LANES - ROPE_DIM // 2, 1), pltpu.roll(xc, ROPE_DIM // 2, 1))
        rc = xc * cos + partner * sin
        za_ref[0, :, c * LANES:(c + 1) * LANES] = rc.astype(BF16)
        if c >= A_WIDTH // LANES:
            kc = c - A_WIDTH // LANES
            km_ref[0, 0, :, kc * LANES:(kc + 1) * LANES] = jnp.mean(rc, axis=0, keepdims=True)
    for c in range(A_WIDTH // LANES):
        vc = za[:, 2 * A_WIDTH + c * LANES:2 * A_WIDTH + (c + 1) * LANES]
        vt_ref[0, c, 0] = vc.T.astype(BF16)

    t = _dot(hb, w_ref[:, C_G:N_COLS]) + gb_ref[...]
    t = GATE_SOFTCAP * jnp.tanh(t / GATE_SOFTCAP)
    logsig = jnp.minimum(t, 0.0) - jnp.log(1.0 + jnp.exp(-jnp.abs(t)))
    zg_ref[0] = jnp.where(lane < M_HEADS, t, logsig)


def _in_projection(x, g, sc, sh, w_perm, conv_w, conv_b, gate_b, cos_t, sin_t):
    b, s, d = x.shape
    tm = IN_TM
    assert tm == MOBA_BLOCK and s % tm == 0
    nt = s // tm
    row = lambda i, j: (i, j, 0)
    bcast = lambda i, j: (i, 0, 0)
    const = lambda i, j: (0, 0)
    out_shape = [
        jax.ShapeDtypeStruct((b, s, 2 * M_WIDTH), BF16),
        jax.ShapeDtypeStruct((b, s, M_WIDTH), BF16),
        jax.ShapeDtypeStruct((b, s, M_WIDTH), BF16),
        jax.ShapeDtypeStruct((b, s, 2 * A_WIDTH), BF16),
        jax.ShapeDtypeStruct((b, A_WIDTH // LANES, nt, LANES, tm), BF16),
        jax.ShapeDtypeStruct((b, s, LANES), F32),
        jax.ShapeDtypeStruct((b, nt, 1, A_WIDTH), F32),
    ]
    out_specs = [
        pl.BlockSpec((1, tm, 2 * M_WIDTH), row),
        pl.BlockSpec((1, tm, M_WIDTH), row),
        pl.BlockSpec((1, tm, M_WIDTH), row),
        pl.BlockSpec((1, tm, 2 * A_WIDTH), row),
        pl.BlockSpec((1, A_WIDTH // LANES, 1, LANES, tm), lambda i, j: (i, 0, j, 0, 0)),
        pl.BlockSpec((1, tm, LANES), row),
        pl.BlockSpec((1, 1, 1, A_WIDTH), lambda i, j: (i, j, 0, 0)),
    ]
    in_specs = [
        pl.BlockSpec((1, tm, d), row),
        pl.BlockSpec((1, d), const),
        pl.BlockSpec((1, 1, d), bcast),
        pl.BlockSpec((1, 1, d), bcast),
        pl.BlockSpec((d, N_COLS), const),
        pl.BlockSpec((CONV_K, 2 * M_WIDTH), const),
        pl.BlockSpec((1, 2 * M_WIDTH), const),
        pl.BlockSpec((1, LANES), const),
        pl.BlockSpec((1, tm, LANES), row),
        pl.BlockSpec((1, tm, LANES), row),
    ]
    return pl.pallas_call(
        _inproj_kernel,
        grid=(b, nt),
        in_specs=in_specs,
        out_specs=out_specs,
        out_shape=out_shape,
        scratch_shapes=[pltpu.VMEM((SUBLANES, 2 * M_WIDTH), F32)],
        compiler_params=_params(("arbitrary", "arbitrary")),
        name="in_proj",
    )(x, g, sc, sh, w_perm, conv_w, conv_b, gate_b, cos_t, sin_t)


def _mlstm_kernel(qk_ref, v_ref, o_ref, g_ref, ng_ref, out_ref, c_scr, n_scr, m_scr):
    L = qk_ref.shape[1]

    @pl.when(pl.program_id(1) == 0)
    def _():
        c_scr[...] = jnp.zeros_like(c_scr)
        n_scr[...] = jnp.zeros_like(n_scr)
        m_scr[...] = jnp.full_like(m_scr, -jnp.inf)

    gates = g_ref[0]
    r = lax.broadcasted_iota(I32, (L, L), 0)
    c = lax.broadcasted_iota(I32, (L, L), 1)
    causal = c <= r
    tri = causal.astype(BF16)
    gh, gl = _split(gates)
    cum = _dot(tri, gh) + _dot(tri, gl)
    lane = lax.broadcasted_iota(I32, gates.shape, 1)
    mix = jnp.where(lane < M_HEADS, gates, cum)
    mix_t = mix.T

    for hd in range(M_HEADS):
        q = qk_ref[0, :, hd * M_HEAD_DIM:(hd + 1) * M_HEAD_DIM]
        k = qk_ref[0, :, M_WIDTH + hd * M_HEAD_DIM:M_WIDTH + (hd + 1) * M_HEAD_DIM]
        v = v_ref[0, :, hd * M_HEAD_DIM:(hd + 1) * M_HEAD_DIM]
        ig_col = mix[:, hd:hd + 1]
        b_col = mix[:, M_HEADS + hd:M_HEADS + hd + 1]
        ig_row = mix_t[hd:hd + 1, :]
        b_row = mix_t[M_HEADS + hd:M_HEADS + hd + 1, :]
        g_tot = b_row[:, L - 1:L]
        m_prev = m_scr[hd, 0:1, 0:1]
        c_prev = c_scr[hd]
        n_prev = n_scr[hd, 0:1, :]

        a_col = b_col + m_prev
        dmat = jnp.where(causal, b_col - b_row + ig_row, -jnp.inf)
        m_t = jnp.maximum(a_col, jnp.max(dmat, axis=1, keepdims=True))
        s = _dot_nt(q, k) * jnp.exp(dmat - m_t)
        inter = jnp.exp(a_col - m_t)
        num = _dot(s.astype(BF16), v) + inter * _dot(q, c_prev.astype(BF16))
        qn = jnp.sum(q.astype(F32) * n_prev, axis=1, keepdims=True)
        den = jnp.sum(s, axis=1, keepdims=True) + inter * qn
        hh = num / jnp.maximum(jnp.abs(den), jnp.exp(-m_t))

        w_row = g_tot - b_row + ig_row
        m_new = jnp.maximum(g_tot + m_prev, jnp.max(w_row, axis=1, keepdims=True))
        decay = jnp.exp(g_tot + m_prev - m_new)
        kw = k.astype(F32) * jnp.exp(g_tot - b_col + ig_col - m_new)
        c_scr[hd] = decay * c_prev + _dot(kw.T.astype(BF16), v)
        n_scr[hd] = jnp.broadcast_to(decay * n_prev + jnp.sum(kw, axis=0, keepdims=True), n_scr.shape[1:])
        m_scr[hd] = jnp.broadcast_to(m_new, m_scr.shape[1:])

        og = _sigmoid(o_ref[0, :, hd * M_HEAD_DIM:(hd + 1) * M_HEAD_DIM].astype(F32))
        ho = hh * og
        hn = ho * lax.rsqrt(jnp.mean(ho * ho, axis=-1, keepdims=True) + NORM_EPS)
        out_ref[0, :, hd * M_HEAD_DIM:(hd + 1) * M_HEAD_DIM] = (
            hn * ng_ref[:, hd * M_HEAD_DIM:(hd + 1) * M_HEAD_DIM]).astype(BF16)


def _mlstm(zqk, zv, zo, zg, norm_g):
    b, s, _ = zqk.shape
    L = min(MLSTM_L, s)
    row = lambda i, j: (i, j, 0)
    return pl.pallas_call(
        _mlstm_kernel,
        grid=(b, s // L),
        in_specs=[pl.BlockSpec((1, L, 2 * M_WIDTH), row),
                  pl.BlockSpec((1, L, M_WIDTH), row),
                  pl.BlockSpec((1, L, M_WIDTH), row),
                  pl.BlockSpec((1, L, LANES), row),
                  pl.BlockSpec((1, M_WIDTH), lambda i, j: (0, 0))],
        out_specs=pl.BlockSpec((1, L, M_WIDTH), row),
        out_shape=jax.ShapeDtypeStruct((b, s, M_WIDTH), BF16),
        scratch_shapes=[pltpu.VMEM((M_HEADS, M_HEAD_DIM, M_HEAD_DIM), F32),
                        pltpu.VMEM((M_HEADS, SUBLANES, M_HEAD_DIM), F32),
                        pltpu.VMEM((M_HEADS, SUBLANES, LANES), F32)],
        compiler_params=_params(("arbitrary", "arbitrary")),
        name="mlstm",
    )(zqk, zv, zo, zg, norm_g)


def _moba_kernel(q_ref, k_ref, vt_ref, km_ref, o_ref, qs_scr, kill_scr, acc_scr):
    tq = q_ref.shape[1]
    nb = km_ref.shape[1]
    own = pl.program_id(1)
    pairs = q_ref.shape[2] // LANES
    per = LANES // A_HEAD_DIM
    chains = [(hp, hd) for hp in range(pairs) for hd in range(per)]

    lane = lax.broadcasted_iota(I32, (tq, LANES), 1)
    head0 = lane < A_HEAD_DIM
    blk = lax.broadcasted_iota(I32, (nb, per * tq), 0)
    for hp in range(pairs):
        q = q_ref[0, :, hp * LANES:(hp + 1) * LANES]
        zero = jnp.zeros_like(q)
        q2 = jnp.concatenate([jnp.where(head0, q, zero), jnp.where(head0, zero, q)], axis=0)
        qs_scr[hp] = (q2.astype(F32) * (A_HEAD_DIM ** -0.5)).astype(BF16)

        kmh, kml = _split(km_ref[0, :, hp * LANES:(hp + 1) * LANES])
        gate = _dot_nt(kmh, q2) + _dot_nt(kml, q2)
        g0 = jnp.where(blk < own, gate, -jnp.inf)
        sel = jnp.zeros(gate.shape, jnp.bool_)
        for _ in range(MOBA_TOPK):
            mx = jnp.max(g0, axis=0, keepdims=True)
            hit = jnp.logical_and(g0 == mx, mx > -jnp.inf)
            idx = jnp.min(jnp.where(hit, blk, nb), axis=0, keepdims=True)
            pick = blk == idx
            sel = jnp.logical_or(sel, pick)
            g0 = jnp.where(pick, -jnp.inf, g0)
        kill_scr[:, hp * per * tq:(hp + 1) * per * tq] = jnp.where(sel, 0.0, -NEG_BIG).astype(F32)

    def scores(kj, hp, hd):
        return _dot_nt(kj[:, hp * LANES:(hp + 1) * LANES], qs_scr[hp, hd * tq:(hd + 1) * tq, :])

    def values(vts, hp, hd, p):
        return _dot(vts[hp][hd * A_HEAD_DIM:(hd + 1) * A_HEAD_DIM, :], p.astype(BF16))

    off = pl.multiple_of(own * tq, tq)
    kj = k_ref[0, pl.ds(off, tq), :]
    vts = [vt_ref[0, hp, own] for hp in range(pairs)]
    key = lax.broadcasted_iota(I32, (tq, tq), 0)
    qry = lax.broadcasted_iota(I32, (tq, tq), 1)
    sts = [jnp.where(key <= qry, scores(kj, hp, hd), NEG_BIG) for hp, hd in chains]
    state = []
    for c, (hp, hd) in enumerate(chains):
        m = jnp.max(sts[c], axis=0, keepdims=True)
        p = jnp.exp(sts[c] - m)
        state += [m, jnp.sum(p, axis=0, keepdims=True)]
        acc_scr[c] = values(vts, hp, hd, p)

    def body(j, carry):
        offj = pl.multiple_of(j * tq, tq)
        kj = k_ref[0, pl.ds(offj, tq), :]
        vts = [vt_ref[0, hp, j] for hp in range(pairs)]
        kill = kill_scr[pl.ds(j, 1), :]
        acc_prev = [acc_scr[c] for c in range(len(chains))]
        sts = [scores(kj, hp, hd) for hp, hd in chains]
        out, alphas, pvs = [], [], []
        for c, (hp, hd) in enumerate(chains):
            m, l = carry[2 * c], carry[2 * c + 1]
            kh = kill[:, c * tq:(c + 1) * tq]
            mn = jnp.maximum(m, jnp.max(sts[c], axis=0, keepdims=True) - kh)
            alpha = jnp.exp(m - mn)
            p = jnp.exp(sts[c] - (mn + kh))
            out += [mn, alpha * l + jnp.sum(p, axis=0, keepdims=True)]
            alphas.append(alpha)
            pvs.append(values(vts, hp, hd, p))
        for c in range(len(chains)):
            acc_scr[c] = alphas[c] * acc_prev[c] + pvs[c]
        return tuple(out)

    state = lax.fori_loop(0, own, body, tuple(state))
    for hp in range(pairs):
        out_t = jnp.concatenate([acc_scr[hp * per + hd] / state[2 * (hp * per + hd) + 1] for hd in range(per)], axis=0)
        o_ref[0, :, hp * LANES:(hp + 1) * LANES] = out_t.T.astype(BF16)


def _moba(za, vt, kmean):
    b, s, _ = za.shape
    tq = MOBA_BLOCK
    nb = s // tq
    pairs = A_WIDTH // LANES
    once = pl.Buffered(1)
    return pl.pallas_call(
        _moba_kernel,
        grid=(b, nb),
        in_specs=[pl.BlockSpec((1, tq, A_WIDTH), lambda i, j: (i, j, 0)),
                  pl.BlockSpec((1, s, A_WIDTH), lambda i, j: (i, 0, 1), pipeline_mode=once),
                  pl.BlockSpec((1, pairs, nb, LANES, tq), lambda i, j: (i, 0, 0, 0, 0), pipeline_mode=once),
                  pl.BlockSpec((1, nb, A_WIDTH), lambda i, j: (i, 0, 0))],
        out_specs=pl.BlockSpec((1, tq, A_WIDTH), lambda i, j: (i, j, 0)),
        out_shape=jax.ShapeDtypeStruct((b, s, A_WIDTH), BF16),
        scratch_shapes=[pltpu.VMEM((pairs, 2 * tq, LANES), BF16),
                        pltpu.VMEM((nb, 2 * pairs * tq), F32),
                        pltpu.VMEM((A_HEADS, A_HEAD_DIM, tq), F32)],
        compiler_params=_params(("arbitrary", "arbitrary")),
        name="moba",
    )(za, za, vt, kmean)


def _outproj_kernel(hm_ref, ha_ref, wo_ref, x_ref, g1_ref, n2_ref, sc_ref, sh_ref, wrh_ref, wrl_ref, br_ref,
                    x1_ref, h2p_ref, ri_ref, rp_ref, cnt_ref, cnt_scr):
    tm = x_ref.shape[1]

    @pl.when(jnp.logical_and(pl.program_id(0) == 0, pl.program_id(1) == 0))
    def _():
        cnt_scr[...] = jnp.zeros_like(cnt_scr)

    y = _dot(hm_ref[0], wo_ref[:M_WIDTH, :]) + _dot(ha_ref[0], wo_ref[M_WIDTH:, :])
    x1 = x_ref[0] + g1_ref[0] * y
    x1_ref[0] = x1
    var = jnp.mean(x1 * x1, axis=-1, keepdims=True)
    h2 = x1 * lax.rsqrt(var + NORM_EPS) * n2_ref[...]
    h2 = h2 * (1.0 + sc_ref[0]) + sh_ref[0]
    h2p_ref[0] = _pack_halves(h2)

    hh, hl = _split(h2)
    logits = _dot(hh, wrh_ref[...]) + (_dot(hl, wrh_ref[...]) + _dot(hh, wrl_ref[...])) + br_ref[...]
    lane = lax.broadcasted_iota(I32, logits.shape, 1)
    g0 = logits
    vals, idxs = [], []
    for _ in range(TOP_K):
        mx = jnp.max(g0, axis=1, keepdims=True)
        idx = jnp.min(jnp.where(g0 == mx, lane, LANES), axis=1, keepdims=True)
        vals.append(mx)
        idxs.append(idx)
        g0 = jnp.where(lane == idx, -jnp.inf, g0)
    ex = [jnp.exp(vv - vals[0]) for vv in vals]
    tot = ex[0] + ex[1] + ex[2] + ex[3]

    onehot = jnp.zeros(logits.shape, F32)
    for idx in idxs:
        onehot = onehot + (lane == idx).astype(F32)
    r = lax.broadcasted_iota(I32, (tm, tm), 0)
    c = lax.broadcasted_iota(I32, (tm, tm), 1)
    before = _dot((c < r).astype(BF16), onehot.astype(BF16)) + cnt_scr[0:1, :]
    ri = jnp.zeros(logits.shape, I32)
    rp = jnp.zeros(logits.shape, F32)
    for kk in range(TOP_K):
        rank = jnp.sum(jnp.where(lane == idxs[kk], before, 0.0), axis=1, keepdims=True)
        ri = jnp.where(lane == kk, idxs[kk], ri)
        ri = jnp.where(lane == TOP_K + kk, rank.astype(I32), ri)
        rp = jnp.where(lane == kk, ex[kk] / tot, rp)
    ri_ref[0] = ri
    rp_ref[0] = rp
    cnt = cnt_scr[...] + jnp.sum(onehot, axis=0, keepdims=True)
    cnt_scr[...] = cnt
    cnt_ref[...] = cnt


def _out_projection(hm, ha, w_out, x, g1, n2g, sc2, sh2, wr_hi, wr_lo, b_r):
    b, s, d = x.shape
    tm = min(OUT_TM, s)
    row = lambda i, j: (i, j, 0)
    bcast = lambda i, j: (i, 0, 0)
    const = lambda i, j: (0, 0)
    out_shape = [
        jax.ShapeDtypeStruct((b, s, d), F32),
        jax.ShapeDtypeStruct((b, s, d // 2), U32),
        jax.ShapeDtypeStruct((b, s, LANES), I32),
        jax.ShapeDtypeStruct((b, s, LANES), F32),
        jax.ShapeDtypeStruct((SUBLANES, LANES), F32),
    ]
    out_specs = [
        pl.BlockSpec((1, tm, d), row),
        pl.BlockSpec((1, tm, d // 2), row),
        pl.BlockSpec((1, tm, LANES), row),
        pl.BlockSpec((1, tm, LANES), row),
        pl.BlockSpec((SUBLANES, LANES), const),
    ]
    in_specs = [
        pl.BlockSpec((1, tm, M_WIDTH), row),
        pl.BlockSpec((1, tm, A_WIDTH), row),
        pl.BlockSpec((M_WIDTH + A_WIDTH, d), const),
        pl.BlockSpec((1, tm, d), row),
        pl.BlockSpec((1, 1, d), bcast),
        pl.BlockSpec((1, d), const),
        pl.BlockSpec((1, 1, d), bcast),
        pl.BlockSpec((1, 1, d), bcast),
        pl.BlockSpec((d, LANES), const),
        pl.BlockSpec((d, LANES), const),
        pl.BlockSpec((1, LANES), const),
    ]
    return pl.pallas_call(
        _outproj_kernel,
        grid=(b, s // tm),
        in_specs=in_specs,
        out_specs=out_specs,
        out_shape=out_shape,
        scratch_shapes=[pltpu.VMEM((SUBLANES, LANES), F32)],
        compiler_params=_params(("arbitrary", "arbitrary")),
        name="out_proj_router",
    )(hm, ha, w_out, x, g1, n2g, sc2, sh2, wr_hi, wr_lo, b_r)


def _row_copy_wait(src, dst, sem):
    pltpu.make_async_copy(src, dst, sem).wait()


def _scatter_kernel(slot_hbm, h2p_ref, xs_in, xs_out, idx_smem, isem, sem):
    del xs_in
    tm = h2p_ref.shape[0]
    i = pl.program_id(0)
    cp = pltpu.make_async_copy(slot_hbm.at[i], idx_smem, isem)
    cp.start()
    cp.wait()

    def body(r, carry):
        for kk in range(TOP_K):
            dst = idx_smem[r * TOP_K + kk]
            pltpu.make_async_copy(h2p_ref.at[pl.ds(r, 1)], xs_out.at[pl.ds(dst, 1)], sem).start()
        return carry

    lax.fori_loop(0, tm, body, 0, unroll=8)
    for kk in range(TOP_K):
        _row_copy_wait(h2p_ref, xs_out.at[pl.ds(0, tm)], sem)


def _scatter_rows(slot_tiles, h2p, n_slots):
    t, w = h2p.shape
    tm = slot_tiles.shape[1] // TOP_K
    xs0 = jnp.zeros((n_slots, w), U32)
    return pl.pallas_call(
        _scatter_kernel,
        grid=(t // tm,),
        in_specs=[pl.BlockSpec(memory_space=pl.ANY),
                  pl.BlockSpec((tm, w), lambda i: (i, 0)),
                  pl.BlockSpec(memory_space=pl.ANY)],
        out_specs=pl.BlockSpec(memory_space=pl.ANY),
        out_shape=jax.ShapeDtypeStruct((n_slots, w), U32),
        scratch_shapes=[pltpu.SMEM((tm * TOP_K,), I32),
                        pltpu.SemaphoreType.DMA(()),
                        pltpu.SemaphoreType.DMA(())],
        input_output_aliases={2: 0},
        compiler_params=_params(("arbitrary",)),
        name="moe_scatter",
    )(slot_tiles, h2p, xs0)


def _expert_kernel(be_ref, bi_ref, nb_ref, xs_ref, wg_ref, bg_ref, wu_ref, bu_ref, wd_ref, bd_ref, yb_ref,
                   wg_scr, wu_scr, wd_scr):
    del bi_ref
    half = xs_ref.shape[1]
    i = pl.program_id(0)

    @pl.when(jnp.logical_or(i == 0, be_ref[i] != be_ref[jnp.maximum(i - 1, 0)]))
    def _():
        wg_scr[...] = wg_ref[0].astype(BF16)
        wu_scr[...] = wu_ref[0].astype(BF16)
        wd_scr[...] = wd_ref[0].astype(BF16)

    @pl.when(i < nb_ref[0])
    def _():
        xp = xs_ref[...]
        xlo = _unpack_lo(xp).astype(BF16)
        xhi = _unpack_hi(xp).astype(BF16)
        gt = _dot(xlo, wg_scr[:half, :]) + _dot(xhi, wg_scr[half:, :]) + bg_ref[0]
        up = _dot(xlo, wu_scr[:half, :]) + _dot(xhi, wu_scr[half:, :]) + bu_ref[0]
        gt = jnp.minimum(gt, SWIGLU_LIMIT)
        up = jnp.clip(up, -SWIGLU_LIMIT, SWIGLU_LIMIT)
        act = (up + 1.0) * (gt * _sigmoid(SWIGLU_ALPHA * gt))
        y = _dot(act.astype(BF16), wd_scr[...]) + bd_ref[0]
        yb_ref[...] = _pack_halves(y)

    @pl.when(pl.program_id(0) >= nb_ref[0])
    def _():
        yb_ref[...] = jnp.zeros_like(yb_ref)


def _experts(layer, block_e, block_i, n_used, xs, wg, bg, wu, bu, wd, bd):
    n_slots, half = xs.shape
    r = EXPERT_R
    depth, ne, d, dff = wg.shape
    wmap = lambda i, be, bi, nb: (layer, be[i], 0, 0)
    xmap = lambda i, be, bi, nb: (bi[i], 0)
    grid_spec = pltpu.PrefetchScalarGridSpec(
        num_scalar_prefetch=3,
        grid=(n_slots // r,),
        in_specs=[pl.BlockSpec((r, half), xmap),
                  pl.BlockSpec((None, 1, d, dff), wmap),
                  pl.BlockSpec((None, 1, 1, dff), wmap),
                  pl.BlockSpec((None, 1, d, dff), wmap),
                  pl.BlockSpec((None, 1, 1, dff), wmap),
                  pl.BlockSpec((None, 1, dff, d), wmap),
                  pl.BlockSpec((None, 1, 1, d), wmap)],
        out_specs=pl.BlockSpec((r, half), lambda i, be, bi, nb: (i, 0)),
        scratch_shapes=[pltpu.VMEM((d, dff), BF16), pltpu.VMEM((d, dff), BF16), pltpu.VMEM((dff, d), BF16)],
    )
    return pl.pallas_call(
        _expert_kernel,
        grid_spec=grid_spec,
        out_shape=jax.ShapeDtypeStruct((n_slots, half), U32),
        compiler_params=_params(("arbitrary",)),
        name="moe_experts",
    )(block_e, block_i, n_used, xs, wg, bg.reshape(depth, ne, 1, dff), wu, bu.reshape(depth, ne, 1, dff),
      wd, bd.reshape(depth, ne, 1, d))


def _combine_kernel(slot_hbm, yb_hbm, rp_ref, x1_ref, g2_ref, fg_ref, out_ref, buf, idx_smem, isem, sem, *, final):
    tm = x1_ref.shape[1]
    half = buf.shape[2]
    i = pl.program_id(0) * pl.num_programs(1) + pl.program_id(1)
    cp = pltpu.make_async_copy(slot_hbm.at[i], idx_smem, isem)
    cp.start()
    cp.wait()

    def body(r, carry):
        for kk in range(TOP_K):
            src = idx_smem[r * TOP_K + kk]
            pltpu.make_async_copy(yb_hbm.at[pl.ds(src, 1)], buf.at[kk, pl.ds(r, 1)], sem).start()
        return carry

    lax.fori_loop(0, tm, body, 0, unroll=8)
    for kk in range(TOP_K):
        _row_copy_wait(yb_hbm.at[pl.ds(0, tm)], buf.at[kk], sem)

    rp = rp_ref[0]
    ylo = jnp.zeros((tm, half), F32)
    yhi = jnp.zeros((tm, half), F32)
    for kk in range(TOP_K):
        w = buf[kk]
        pk = rp[:, kk:kk + 1]
        ylo = ylo + pk * _unpack_lo(w)
        yhi = yhi + pk * _unpack_hi(w)
    x2 = x1_ref[0] + g2_ref[0] * jnp.concatenate([ylo, yhi], axis=1)
    if final:
        x2 = x2 * lax.rsqrt(jnp.mean(x2 * x2, axis=-1, keepdims=True) + NORM_EPS) * fg_ref[...]
    out_ref[0] = x2


def _combine(slot_tiles, yb, rp, x1, g2, fg, final):
    b, s, d = x1.shape
    tm = slot_tiles.shape[1] // TOP_K
    row = lambda i, j: (i, j, 0)
    return pl.pallas_call(
        functools.partial(_combine_kernel, final=final),
        grid=(b, s // tm),
        in_specs=[pl.BlockSpec(memory_space=pl.ANY),
                  pl.BlockSpec(memory_space=pl.ANY),
                  pl.BlockSpec((1, tm, LANES), row),
                  pl.BlockSpec((1, tm, d), row),
                  pl.BlockSpec((1, 1, d), lambda i, j: (i, 0, 0)),
                  pl.BlockSpec((1, d), lambda i, j: (0, 0))],
        out_specs=pl.BlockSpec((1, tm, d), row),
        out_shape=jax.ShapeDtypeStruct((b, s, d), F32),
        scratch_shapes=[pltpu.VMEM((TOP_K, tm, d // 2), U32),
                        pltpu.SMEM((tm * TOP_K,), I32),
                        pltpu.SemaphoreType.DMA(()),
                        pltpu.SemaphoreType.DMA(())],
        compiler_params=_params(("arbitrary", "arbitrary")),
        name="moe_combine",
    )(slot_tiles, yb, rp, x1, g2, fg)


def _permute_w_in(w_in):
    d = w_in.shape[0]
    g0 = 4 * M_WIDTH
    a0 = g0 + 2 * M_HEADS
    pad = jnp.zeros((d, LANES - 2 * M_HEADS), w_in.dtype)
    return jnp.concatenate([w_in[:, :g0], w_in[:, a0:], w_in[:, g0:a0], pad], axis=1).astype(BF16)


def kernel(x, c, positions, w_ada, b_ada, norm1_g, w_in, conv_w, conv_b, b_igate, b_fgate, mlstm_norm_g,
           w_out, norm2_g, w_router, b_router, w_gate, b_gate, w_up, b_up, w_down, b_down, final_norm_g):
    b, s, d = x.shape
    depth = w_ada.shape[0]
    t = b * s
    tm = min(MOE_TM, s)
    n_slots = (t * TOP_K // EXPERT_R + N_EXPERTS) * EXPERT_R
    n_blocks = n_slots // EXPERT_R

    mod = _modulation(c, w_ada, b_ada)
    cos_t, sin_t = _rope_tables(positions)

    for l in range(depth):
        sh1, sc1, g1, sh2, sc2, g2 = [mod[l, :, i] for i in range(6)]
        gate_b = jnp.zeros((1, LANES), F32).at[0, :M_HEADS].set(b_igate[l]).at[0, M_HEADS:2 * M_HEADS].set(b_fgate[l])
        zqk, zv, zo, za, vt, zg, kmean = _in_projection(
            x, norm1_g[l].reshape(1, d), sc1, sh1, _permute_w_in(w_in[l]), conv_w[l],
            conv_b[l].reshape(1, -1), gate_b, cos_t, sin_t)
        hm = _mlstm(zqk, zv, zo, zg, mlstm_norm_g[l].reshape(1, -1))
        ha = _moba(za, vt, kmean.reshape(b, s // MOBA_BLOCK, A_WIDTH))

        wr = jnp.zeros((d, LANES), F32).at[:, :N_EXPERTS].set(w_router[l])
        wr_hi = wr.astype(BF16)
        wr_lo = (wr - wr_hi.astype(F32)).astype(BF16)
        b_r = jnp.full((1, LANES), NEG_BIG, F32).at[0, :N_EXPERTS].set(b_router[l])
        x1, h2p, ri, rp, cnt = _out_projection(
            hm, ha, w_out[l].astype(BF16), x, g1, norm2_g[l].reshape(1, d), sc2, sh2, wr_hi, wr_lo, b_r)

        counts = cnt[0, :N_EXPERTS].astype(I32)
        nblk = (counts + EXPERT_R - 1) // EXPERT_R
        blk_end = jnp.cumsum(nblk)
        blk_start = blk_end - nblk
        n_used = blk_end[-1:]
        block_i = jnp.minimum(jnp.arange(n_blocks, dtype=I32), n_used[0] - 1)
        block_e = jnp.minimum(jnp.sum((blk_end[None, :] <= block_i[:, None]).astype(I32), axis=1), N_EXPERTS - 1)
        eid = ri[..., :TOP_K].reshape(t, TOP_K)
        rank = ri[..., TOP_K:2 * TOP_K].reshape(t, TOP_K)
        slot = blk_start[eid] * EXPERT_R + rank
        slot_tiles = slot.reshape(t // tm, tm * TOP_K)

        xs = _scatter_rows(slot_tiles, h2p.reshape(t, d // 2), n_slots)
        yb = _experts(l, block_e, block_i, n_used.astype(I32), xs, w_gate, b_gate, w_up, b_up, w_down, b_down)
        x = _combine(slot_tiles, yb, rp, x1, g2, final_norm_g.reshape(1, d), final=(l == depth - 1))
    return x
```

```python
import functools

import jax
import jax.numpy as jnp
from jax import lax
from jax.experimental import pallas as pl
from jax.experimental.pallas import tpu as pltpu

F32 = jnp.float32
BF16 = jnp.bfloat16
I32 = jnp.int32
U32 = jnp.uint32

M_HEADS = 4
M_HEAD_DIM = 128
M_WIDTH = M_HEADS * M_HEAD_DIM
A_HEADS = 8
A_HEAD_DIM = 64
A_WIDTH = A_HEADS * A_HEAD_DIM
CONV_K = 4
GATE_SOFTCAP = 15.0
MOBA_BLOCK = 256
MOBA_TOPK = 3
ROPE_THETA = 500000.0
ROPE_DIM = A_HEAD_DIM // 4
N_EXPERTS = 32
TOP_K = 4
SWIGLU_LIMIT = 7.0
SWIGLU_ALPHA = 1.702
NORM_EPS = 1e-5

LANES = 128
SUBLANES = 8
VMEM_LIMIT = 56 * 1024 * 1024

NEG_BIG = -1e30
MLSTM_L = 256
IN_TM = 256
OUT_TM = 512
MOE_TM = 512
EXPERT_R = 512
VT_ROWS = A_HEAD_DIM + 16
LOG2E = 1.4426950408889634

C_QK = 0
C_V = 2 * M_WIDTH
C_O = 3 * M_WIDTH
C_A = 4 * M_WIDTH
C_G = 4 * M_WIDTH + 3 * A_WIDTH
N_COLS = C_G + LANES


def _dot(a, b):
    return jnp.dot(a, b, preferred_element_type=F32)


def _dot_nt(a, b):
    return lax.dot_general(a, b, (((1,), (1,)), ((), ())), preferred_element_type=F32)


def _split(a):
    hi = a.astype(BF16)
    lo = (a - hi.astype(F32)).astype(BF16)
    return hi, lo


def _dot3(a, b):
    ah, al = _split(a)
    bh, bl = _split(b)
    return _dot(ah, bh) + (_dot(al, bh) + _dot(ah, bl))


def _sigmoid(t):
    return 1.0 / (1.0 + jnp.exp(-t))


def _params(sem):
    return pltpu.CompilerParams(dimension_semantics=sem, vmem_limit_bytes=VMEM_LIMIT)


def _pack_halves(t):
    w = t.shape[1] // 2
    lo = pltpu.bitcast(t[:, :w].astype(BF16).astype(F32), U32)
    hi = pltpu.bitcast(t[:, w:].astype(BF16).astype(F32), U32)
    return (lo >> 16) | (hi & jnp.uint32(0xFFFF0000))


def _unpack_lo(p):
    return pltpu.bitcast(p << 16, F32)


def _unpack_hi(p):
    return pltpu.bitcast(p & jnp.uint32(0xFFFF0000), F32)


def _mod_kernel(c_ref, w_ref, b_ref, o_ref):
    c = c_ref[...]
    cond = c * _sigmoid(c)
    o_ref[0] = _dot3(cond, w_ref[0]) + b_ref[0]


def _modulation(c, w_ada, b_ada):
    depth, d, n6 = w_ada.shape
    b = c.shape[0]
    rows = -(-b // SUBLANES) * SUBLANES
    cp = jnp.zeros((rows, d), F32).at[:b].set(c)
    tn = 1536
    out = pl.pallas_call(
        _mod_kernel,
        grid=(depth, n6 // tn),
        in_specs=[pl.BlockSpec((rows, d), lambda l, j: (0, 0)),
                  pl.BlockSpec((1, d, tn), lambda l, j: (l, 0, j)),
                  pl.BlockSpec((1, 1, tn), lambda l, j: (l, 0, j))],
        out_specs=pl.BlockSpec((1, rows, tn), lambda l, j: (l, 0, j)),
        out_shape=jax.ShapeDtypeStruct((depth, rows, n6), F32),
        compiler_params=_params(("arbitrary", "arbitrary")),
        name="adaln_mod",
    )(cp, w_ada, b_ada.reshape(depth, 1, n6))
    return out[:, :b].reshape(depth, b, 6, 1, d)


def _rope_kernel(pos_ref, invf_ref, sign_ref, cos_ref, sin_ref):
    ang = pos_ref[0].astype(F32) * invf_ref[...]
    cos_ref[0] = jnp.cos(ang)
    sin_ref[0] = jnp.sin(ang) * sign_ref[...]


def _rope_tables(positions):
    b, s = positions.shape
    half = ROPE_DIM // 2
    inv_freq = ROPE_THETA ** (-jnp.arange(0, ROPE_DIM, 2, dtype=F32) / ROPE_DIM)
    lane = jnp.arange(LANES) % A_HEAD_DIM
    invf = jnp.where(lane < ROPE_DIM, inv_freq[lane % half], 0.0).astype(F32).reshape(1, LANES)
    sign = jnp.where(lane < half, -1.0, jnp.where(lane < ROPE_DIM, 1.0, 0.0)).astype(F32).reshape(1, LANES)
    ts = min(s, 1024)
    tab = jax.ShapeDtypeStruct((b, s, LANES), F32)
    return pl.pallas_call(
        _rope_kernel,
        grid=(b, s // ts),
        in_specs=[pl.BlockSpec((1, ts, 1), lambda i, j: (i, j, 0)),
                  pl.BlockSpec((1, LANES), lambda i, j: (0, 0)),
                  pl.BlockSpec((1, LANES), lambda i, j: (0, 0))],
        out_specs=[pl.BlockSpec((1, ts, LANES), lambda i, j: (i, j, 0))] * 2,
        out_shape=[tab, tab],
        compiler_params=_params(("arbitrary", "arbitrary")),
        name="rope_tables",
    )(positions.reshape(b, s, 1), invf, sign)


def _inproj_kernel(x_ref, g_ref, sc_ref, sh_ref, w_ref, cw_ref, cb_ref, gb_ref, cos_ref, sin_ref,
                   zqk_ref, zv_ref, zo_ref, za_ref, vt_ref, zg_ref, km_ref, halo_ref):
    tm = x_ref.shape[1]
    x = x_ref[0]
    var = jnp.mean(x * x, axis=-1, keepdims=True)
    h = x * lax.rsqrt(var + NORM_EPS) * g_ref[...]
    h = h * (1.0 + sc_ref[0]) + sh_ref[0]
    hb = h.astype(BF16)

    @pl.when(pl.program_id(1) == 0)
    def _():
        halo_ref[...] = jnp.zeros_like(halo_ref)

    z = _dot(hb, w_ref[:, C_QK:C_V])
    prev = halo_ref[...]
    row8 = lax.broadcasted_iota(I32, prev.shape, 0)
    acc = z * cw_ref[CONV_K - 1:CONV_K, :] + cb_ref[...]
    for k in range(1, CONV_K):
        zs = pltpu.roll(z, k, 0)
        ps = pltpu.roll(prev, k, 0)
        head = jnp.where(row8 < k, ps, zs[:SUBLANES])
        zs = jnp.concatenate([head, zs[SUBLANES:]], axis=0)
        acc = acc + zs * cw_ref[CONV_K - 1 - k:CONV_K - k, :]
    halo_ref[...] = z[tm - SUBLANES:, :]
    qk = acc * _sigmoid(acc)
    col = lax.broadcasted_iota(I32, (1, 2 * M_WIDTH), 1)
    kscale = jnp.where(col < M_WIDTH, 1.0, M_HEAD_DIM ** -0.5).astype(F32)
    zqk_ref[0] = (qk * kscale).astype(BF16)

    zv_ref[0] = _dot(hb, w_ref[:, C_V:C_O]).astype(BF16)
    zo_ref[0] = _dot(hb, w_ref[:, C_O:C_A]).astype(BF16)

    za = _dot(hb, w_ref[:, C_A:C_G])
    cos = cos_ref[0]
    sin = sin_ref[0]
    lane = lax.broadcasted_iota(I32, (tm, LANES), 1)
    first_half = (lane % A_HEAD_DIM) < (ROPE_DIM // 2)
    for c in range(2 * A_WIDTH // LANES):
        xc = za[:, c * LANES:(c + 1) * LANES]
        partner = jnp.where(first_half, pltpu.roll(xc, LANES - ROPE_DIM // 2, 1), pltpu.roll(xc, ROPE_DIM // 2, 1))
        rc = xc * cos + partner * sin
        za_ref[0, :, c * LANES:(c + 1) * LANES] = rc.astype(BF16)
        if c >= A_WIDTH // LANES:
            kc = c - A_WIDTH // LANES
            km_ref[0, 0, :, kc * LANES:(kc + 1) * LANES] = jnp.mean(rc, axis=0, keepdims=True)
    for c in range(A_WIDTH // LANES):
        vc = za[:, 2 * A_WIDTH + c * LANES:2 * A_WIDTH + (c + 1) * LANES]
        vct = vc.T
        extra = (lax.broadcasted_iota(I32, (VT_ROWS - A_HEAD_DIM, tm), 0) == 0).astype(F32)
        for hd in range(LANES // A_HEAD_DIM):
            head_t = vct[hd * A_HEAD_DIM:(hd + 1) * A_HEAD_DIM, :]
            vt_ref[0, c * (LANES // A_HEAD_DIM) + hd, 0] = jnp.concatenate([head_t, extra], axis=0).astype(BF16)

    t = _dot(hb, w_ref[:, C_G:N_COLS]) + gb_ref[...]
    t = GATE_SOFTCAP * jnp.tanh(t / GATE_SOFTCAP)
    logsig = jnp.minimum(t, 0.0) - jnp.log(1.0 + jnp.exp(-jnp.abs(t)))
    zg_ref[0] = jnp.where(lane < M_HEADS, t, logsig)


def _in_projection(x, g, sc, sh, w_perm, conv_w, conv_b, gate_b, cos_t, sin_t):
    b, s, d = x.shape
    tm = IN_TM
    assert tm == MOBA_BLOCK and s % tm == 0
    nt = s // tm
    row = lambda i, j: (i, j, 0)
    bcast = lambda i, j: (i, 0, 0)
    const = lambda i, j: (0, 0)
    out_shape = [
        jax.ShapeDtypeStruct((b, s, 2 * M_WIDTH), BF16),
        jax.ShapeDtypeStruct((b, s, M_WIDTH), BF16),
        jax.ShapeDtypeStruct((b, s, M_WIDTH), BF16),
        jax.ShapeDtypeStruct((b, s, 2 * A_WIDTH), BF16),
        jax.ShapeDtypeStruct((b, A_HEADS, nt, VT_ROWS, tm), BF16),
        jax.ShapeDtypeStruct((b, s, LANES), F32),
        jax.ShapeDtypeStruct((b, nt, 1, A_WIDTH), F32),
    ]
    out_specs = [
        pl.BlockSpec((1, tm, 2 * M_WIDTH), row),
        pl.BlockSpec((1, tm, M_WIDTH), row),
        pl.BlockSpec((1, tm, M_WIDTH), row),
        pl.BlockSpec((1, tm, 2 * A_WIDTH), row),
        pl.BlockSpec((1, A_HEADS, 1, VT_ROWS, tm), lambda i, j: (i, 0, j, 0, 0)),
        pl.BlockSpec((1, tm, LANES), row),
        pl.BlockSpec((1, 1, 1, A_WIDTH), lambda i, j: (i, j, 0, 0)),
    ]
    in_specs = [
        pl.BlockSpec((1, tm, d), row),
        pl.BlockSpec((1, d), const),
        pl.BlockSpec((1, 1, d), bcast),
        pl.BlockSpec((1, 1, d), bcast),
        pl.BlockSpec((d, N_COLS), const),
        pl.BlockSpec((CONV_K, 2 * M_WIDTH), const),
        pl.BlockSpec((1, 2 * M_WIDTH), const),
        pl.BlockSpec((1, LANES), const),
        pl.BlockSpec((1, tm, LANES), row),
        pl.BlockSpec((1, tm, LANES), row),
    ]
    return pl.pallas_call(
        _inproj_kernel,
        grid=(b, nt),
        in_specs=in_specs,
        out_specs=out_specs,
        out_shape=out_shape,
        scratch_shapes=[pltpu.VMEM((SUBLANES, 2 * M_WIDTH), F32)],
        compiler_params=_params(("arbitrary", "arbitrary")),
        name="in_proj",
    )(x, g, sc, sh, w_perm, conv_w, conv_b, gate_b, cos_t, sin_t)


def _mlstm_kernel(qk_ref, v_ref, o_ref, g_ref, ng_ref, out_ref, c_scr, n_scr, m_scr):
    L = qk_ref.shape[1]

    @pl.when(pl.program_id(1) == 0)
    def _():
        c_scr[...] = jnp.zeros_like(c_scr)
        n_scr[...] = jnp.zeros_like(n_scr)
        m_scr[...] = jnp.full_like(m_scr, -jnp.inf)

    gates = g_ref[0]
    r = lax.broadcasted_iota(I32, (L, L), 0)
    c = lax.broadcasted_iota(I32, (L, L), 1)
    causal = c <= r
    tri = causal.astype(BF16)
    gh, gl = _split(gates)
    cum = _dot(tri, gh) + _dot(tri, gl)
    lane = lax.broadcasted_iota(I32, gates.shape, 1)
    mix = jnp.where(lane < M_HEADS, gates, cum)
    mix_t = mix.T

    for hd in range(M_HEADS):
        q = qk_ref[0, :, hd * M_HEAD_DIM:(hd + 1) * M_HEAD_DIM]
        k = qk_ref[0, :, M_WIDTH + hd * M_HEAD_DIM:M_WIDTH + (hd + 1) * M_HEAD_DIM]
        v = v_ref[0, :, hd * M_HEAD_DIM:(hd + 1) * M_HEAD_DIM]
        ig_col = mix[:, hd:hd + 1]
        b_col = mix[:, M_HEADS + hd:M_HEADS + hd + 1]
        ig_row = mix_t[hd:hd + 1, :]
        b_row = mix_t[M_HEADS + hd:M_HEADS + hd + 1, :]
        g_tot = b_row[:, L - 1:L]
        m_prev = m_scr[hd, 0:1, 0:1]
        c_prev = c_scr[hd]
        n_prev = n_scr[hd, 0:1, :]

        a_col = b_col + m_prev
        dmat = jnp.where(causal, b_col - b_row + ig_row, -jnp.inf)
        m_t = jnp.maximum(a_col, jnp.max(dmat, axis=1, keepdims=True))
        s = _dot_nt(q, k) * jnp.exp(dmat - m_t)
        inter = jnp.exp(a_col - m_t)
        num = _dot(s.astype(BF16), v) + inter * _dot(q, c_prev.astype(BF16))
        qn = jnp.sum(q.astype(F32) * n_prev, axis=1, keepdims=True)
        den = jnp.sum(s, axis=1, keepdims=True) + inter * qn
        hh = num / jnp.maximum(jnp.abs(den), jnp.exp(-m_t))

        w_row = g_tot - b_row + ig_row
        m_new = jnp.maximum(g_tot + m_prev, jnp.max(w_row, axis=1, keepdims=True))
        decay = jnp.exp(g_tot + m_prev - m_new)
        kw = k.astype(F32) * jnp.exp(g_tot - b_col + ig_col - m_new)
        c_scr[hd] = decay * c_prev + _dot(kw.T.astype(BF16), v)
        n_scr[hd] = jnp.broadcast_to(decay * n_prev + jnp.sum(kw, axis=0, keepdims=True), n_scr.shape[1:])
        m_scr[hd] = jnp.broadcast_to(m_new, m_scr.shape[1:])

        og = _sigmoid(o_ref[0, :, hd * M_HEAD_DIM:(hd + 1) * M_HEAD_DIM].astype(F32))
        ho = hh * og
        hn = ho * lax.rsqrt(jnp.mean(ho * ho, axis=-1, keepdims=True) + NORM_EPS)
        out_ref[0, :, hd * M_HEAD_DIM:(hd + 1) * M_HEAD_DIM] = (
            hn * ng_ref[:, hd * M_HEAD_DIM:(hd + 1) * M_HEAD_DIM]).astype(BF16)


def _mlstm(zqk, zv, zo, zg, norm_g):
    b, s, _ = zqk.shape
    L = min(MLSTM_L, s)
    row = lambda i, j: (i, j, 0)
    return pl.pallas_call(
        _mlstm_kernel,
        grid=(b, s // L),
        in_specs=[pl.BlockSpec((1, L, 2 * M_WIDTH), row),
                  pl.BlockSpec((1, L, M_WIDTH), row),
                  pl.BlockSpec((1, L, M_WIDTH), row),
                  pl.BlockSpec((1, L, LANES), row),
                  pl.BlockSpec((1, M_WIDTH), lambda i, j: (0, 0))],
        out_specs=pl.BlockSpec((1, L, M_WIDTH), row),
        out_shape=jax.ShapeDtypeStruct((b, s, M_WIDTH), BF16),
        scratch_shapes=[pltpu.VMEM((M_HEADS, M_HEAD_DIM, M_HEAD_DIM), F32),
                        pltpu.VMEM((M_HEADS, SUBLANES, M_HEAD_DIM), F32),
                        pltpu.VMEM((M_HEADS, SUBLANES, LANES), F32)],
        compiler_params=_params(("arbitrary", "arbitrary")),
        name="mlstm",
    )(zqk, zv, zo, zg, norm_g)


def _moba_kernel(q_ref, k_ref, vt_ref, km_ref, o_ref, qs_scr, kill_scr, acc_scr, st_scr, st2_scr):
    tq = q_ref.shape[1]
    nb = km_ref.shape[1]
    own = pl.program_id(1)
    pairs = q_ref.shape[2] // LANES
    per = LANES // A_HEAD_DIM
    chains = [(hp, hd) for hp in range(pairs) for hd in range(per)]

    lane = lax.broadcasted_iota(I32, (tq, LANES), 1)
    head0 = lane < A_HEAD_DIM
    blk = lax.broadcasted_iota(I32, (nb, per * tq), 0)
    for hp in range(pairs):
        q = q_ref[0, :, hp * LANES:(hp + 1) * LANES]
        zero = jnp.zeros_like(q)
        q2 = jnp.concatenate([jnp.where(head0, q, zero), jnp.where(head0, zero, q)], axis=0)
        qs_scr[hp] = (q2.astype(F32) * (A_HEAD_DIM ** -0.5 * LOG2E)).astype(BF16)

        kmh, kml = _split(km_ref[0, :, hp * LANES:(hp + 1) * LANES])
        gate = _dot_nt(kmh, q2) + _dot_nt(kml, q2)
        g0 = jnp.where(blk < own, gate, -jnp.inf)
        sel = jnp.zeros(gate.shape, jnp.bool_)
        for _ in range(MOBA_TOPK):
            mx = jnp.max(g0, axis=0, keepdims=True)
            hit = jnp.logical_and(g0 == mx, mx > -jnp.inf)
            idx = jnp.min(jnp.where(hit, blk, nb), axis=0, keepdims=True)
            pick = blk == idx
            sel = jnp.logical_or(sel, pick)
            g0 = jnp.where(pick, -jnp.inf, g0)
        kill_scr[:, hp * per * tq:(hp + 1) * per * tq] = jnp.where(sel, 0.0, -NEG_BIG).astype(F32)

    def scores(kj, hp, hd):
        return _dot_nt(kj[:, hp * LANES:(hp + 1) * LANES], qs_scr[hp, hd * tq:(hd + 1) * tq, :])

    def values(j, c, p):
        return _dot(vt_ref[0, c, j], p.astype(BF16))

    off = pl.multiple_of(own * tq, tq)
    kj = k_ref[0, pl.ds(off, tq), :]
    key = lax.broadcasted_iota(I32, (tq, tq), 0)
    qry = lax.broadcasted_iota(I32, (tq, tq), 1)
    sts = [jnp.where(key <= qry, scores(kj, hp, hd), NEG_BIG) for hp, hd in chains]
    state = []
    for c in range(len(chains)):
        m = jnp.max(sts[c], axis=0, keepdims=True)
        state.append(m)
        acc_scr[c] = values(own, c, jnp.exp2(sts[c] - m))

    def stage_scores(j, dst):
        offj = pl.multiple_of(j * tq, tq)
        kj = k_ref[0, pl.ds(offj, tq), :]
        for c, (hp, hd) in enumerate(chains):
            dst[c] = scores(kj, hp, hd)

    def step(j, src, dst, carry):
        stage_scores(j + 1, dst)
        kill = kill_scr[pl.ds(j, 1), :]
        acc_prev = [acc_scr[c] for c in range(len(chains))]
        out, alphas, pvs = [], [], []
        for c in range(len(chains)):
            m = carry[c]
            kh = kill[:, c * tq:(c + 1) * tq]
            mn = jnp.maximum(m, jnp.max(src[c], axis=0, keepdims=True) - kh)
            out.append(mn)
            alphas.append(jnp.exp2(m - mn))
            pvs.append(values(j, c, jnp.exp2(src[c] - (mn + kh))))
        for c in range(len(chains)):
            acc_scr[c] = alphas[c] * acc_prev[c] + pvs[c]
        return tuple(out)

    stage_scores(0, st_scr)

    def body(i, carry):
        carry = step(2 * i, st_scr, st2_scr, carry)
        return step(2 * i + 1, st2_scr, st_scr, carry)

    state = lax.fori_loop(0, own // 2, body, tuple(state))
    lax.cond(own % 2 == 1, lambda st: step(own - 1, st_scr, st2_scr, st), lambda st: st, state)
    for hp in range(pairs):
        heads_t = []
        for hd in range(per):
            acc = acc_scr[hp * per + hd]
            heads_t.append(acc[:A_HEAD_DIM] / acc[A_HEAD_DIM:A_HEAD_DIM + 1])
        o_ref[0, :, hp * LANES:(hp + 1) * LANES] = jnp.concatenate(heads_t, axis=0).T.astype(BF16)


def _moba(za, vt, kmean):
    b, s, _ = za.shape
    tq = MOBA_BLOCK
    nb = s // tq
    pairs = A_WIDTH // LANES
    once = pl.Buffered(1)
    return pl.pallas_call(
        _moba_kernel,
        grid=(b, nb),
        in_specs=[pl.BlockSpec((1, tq, A_WIDTH), lambda i, j: (i, j, 0)),
                  pl.BlockSpec((1, s, A_WIDTH), lambda i, j: (i, 0, 1), pipeline_mode=once),
                  pl.BlockSpec((1, A_HEADS, nb, VT_ROWS, tq), lambda i, j: (i, 0, 0, 0, 0), pipeline_mode=once),
                  pl.BlockSpec((1, nb, A_WIDTH), lambda i, j: (i, 0, 0))],
        out_specs=pl.BlockSpec((1, tq, A_WIDTH), lambda i, j: (i, j, 0)),
        out_shape=jax.ShapeDtypeStruct((b, s, A_WIDTH), BF16),
        scratch_shapes=[pltpu.VMEM((pairs, 2 * tq, LANES), BF16),
                        pltpu.VMEM((nb, 2 * pairs * tq), F32),
                        pltpu.VMEM((A_HEADS, VT_ROWS, tq), F32),
                        pltpu.VMEM((A_HEADS, tq, tq), F32),
                        pltpu.VMEM((A_HEADS, tq, tq), F32)],
        compiler_params=_params(("arbitrary", "arbitrary")),
        name="moba",
    )(za, za, vt, kmean)


def _outproj_kernel(hm_ref, ha_ref, wo_ref, x_ref, g1_ref, n2_ref, sc_ref, sh_ref, wrh_ref, wrl_ref, br_ref,
                    x1_ref, h2p_ref, ri_ref, rp_ref, cnt_ref, cnt_scr):
    tm = x_ref.shape[1]

    @pl.when(jnp.logical_and(pl.program_id(0) == 0, pl.program_id(1) == 0))
    def _():
        cnt_scr[...] = jnp.zeros_like(cnt_scr)

    y = _dot(hm_ref[0], wo_ref[:M_WIDTH, :]) + _dot(ha_ref[0], wo_ref[M_WIDTH:, :])
    x1 = x_ref[0] + g1_ref[0] * y
    x1_ref[0] = x1
    var = jnp.mean(x1 * x1, axis=-1, keepdims=True)
    h2 = x1 * lax.rsqrt(var + NORM_EPS) * n2_ref[...]
    h2 = h2 * (1.0 + sc_ref[0]) + sh_ref[0]
    h2p_ref[0] = _pack_halves(h2)

    hh, hl = _split(h2)
    logits = _dot(hh, wrh_ref[...]) + (_dot(hl, wrh_ref[...]) + _dot(hh, wrl_ref[...])) + br_ref[...]
    lane = lax.broadcasted_iota(I32, logits.shape, 1)
    g0 = logits
    vals, idxs = [], []
    for _ in range(TOP_K):
        mx = jnp.max(g0, axis=1, keepdims=True)
        idx = jnp.min(jnp.where(g0 == mx, lane, LANES), axis=1, keepdims=True)
        vals.append(mx)
        idxs.append(idx)
        g0 = jnp.where(lane == idx, -jnp.inf, g0)
    ex = [jnp.exp(vv - vals[0]) for vv in vals]
    tot = ex[0] + ex[1] + ex[2] + ex[3]

    onehot = jnp.zeros(logits.shape, F32)
    for idx in idxs:
        onehot = onehot + (lane == idx).astype(F32)
    r = lax.broadcasted_iota(I32, (tm, tm), 0)
    c = lax.broadcasted_iota(I32, (tm, tm), 1)
    before = _dot((c < r).astype(BF16), onehot.astype(BF16)) + cnt_scr[0:1, :]
    ri = jnp.zeros(logits.shape, I32)
    rp = jnp.zeros(logits.shape, F32)
    for kk in range(TOP_K):
        rank = jnp.sum(jnp.where(lane == idxs[kk], before, 0.0), axis=1, keepdims=True)
        ri = jnp.where(lane == kk, idxs[kk], ri)
        ri = jnp.where(lane == TOP_K + kk, rank.astype(I32), ri)
        rp = jnp.where(lane == kk, ex[kk] / tot, rp)
    ri_ref[0] = ri
    rp_ref[0] = rp
    cnt = cnt_scr[...] + jnp.sum(onehot, axis=0, keepdims=True)
    cnt_scr[...] = cnt
    cnt_ref[...] = cnt


def _out_projection(hm, ha, w_out, x, g1, n2g, sc2, sh2, wr_hi, wr_lo, b_r):
    b, s, d = x.shape
    tm = min(OUT_TM, s)
    row = lambda i, j: (i, j, 0)
    bcast = lambda i, j: (i, 0, 0)
    const = lambda i, j: (0, 0)
    out_shape = [
        jax.ShapeDtypeStruct((b, s, d), F32),
        jax.ShapeDtypeStruct((b, s, d // 2), U32),
        jax.ShapeDtypeStruct((b, s, LANES), I32),
        jax.ShapeDtypeStruct((b, s, LANES), F32),
        jax.ShapeDtypeStruct((SUBLANES, LANES), F32),
    ]
    out_specs = [
        pl.BlockSpec((1, tm, d), row),
        pl.BlockSpec((1, tm, d // 2), row),
        pl.BlockSpec((1, tm, LANES), row),
        pl.BlockSpec((1, tm, LANES), row),
        pl.BlockSpec((SUBLANES, LANES), const),
    ]
    in_specs = [
        pl.BlockSpec((1, tm, M_WIDTH), row),
        pl.BlockSpec((1, tm, A_WIDTH), row),
        pl.BlockSpec((M_WIDTH + A_WIDTH, d), const),
        pl.BlockSpec((1, tm, d), row),
        pl.BlockSpec((1, 1, d), bcast),
        pl.BlockSpec((1, d), const),
        pl.BlockSpec((1, 1, d), bcast),
        pl.BlockSpec((1, 1, d), bcast),
        pl.BlockSpec((d, LANES), const),
        pl.BlockSpec((d, LANES), const),
        pl.BlockSpec((1, LANES), const),
    ]
    return pl.pallas_call(
        _outproj_kernel,
        grid=(b, s // tm),
        in_specs=in_specs,
        out_specs=out_specs,
        out_shape=out_shape,
        scratch_shapes=[pltpu.VMEM((SUBLANES, LANES), F32)],
        compiler_params=_params(("arbitrary", "arbitrary")),
        name="out_proj_router",
    )(hm, ha, w_out, x, g1, n2g, sc2, sh2, wr_hi, wr_lo, b_r)


def _row_copy_wait(src, dst, sem):
    pltpu.make_async_copy(src, dst, sem).wait()


def _scatter_kernel(slot_hbm, h2p_ref, xs_in, xs_out, idx_smem, isem, sem):
    del xs_in
    tm = h2p_ref.shape[0]
    i = pl.program_id(0)
    cp = pltpu.make_async_copy(slot_hbm.at[i], idx_smem, isem)
    cp.start()
    cp.wait()

    def body(r, carry):
        for kk in range(TOP_K):
            dst = idx_smem[r * TOP_K + kk]
            pltpu.make_async_copy(h2p_ref.at[pl.ds(r, 1)], xs_out.at[pl.ds(dst, 1)], sem).start()
        return carry

    lax.fori_loop(0, tm, body, 0, unroll=8)
    for kk in range(TOP_K):
        _row_copy_wait(h2p_ref, xs_out.at[pl.ds(0, tm)], sem)


def _scatter_rows(slot_tiles, h2p, n_slots):
    t, w = h2p.shape
    tm = slot_tiles.shape[1] // TOP_K
    xs0 = jnp.zeros((n_slots, w), U32)
    return pl.pallas_call(
        _scatter_kernel,
        grid=(t // tm,),
        in_specs=[pl.BlockSpec(memory_space=pl.ANY),
                  pl.BlockSpec((tm, w), lambda i: (i, 0)),
                  pl.BlockSpec(memory_space=pl.ANY)],
        out_specs=pl.BlockSpec(memory_space=pl.ANY),
        out_shape=jax.ShapeDtypeStruct((n_slots, w), U32),
        scratch_shapes=[pltpu.SMEM((tm * TOP_K,), I32),
                        pltpu.SemaphoreType.DMA(()),
                        pltpu.SemaphoreType.DMA(())],
        input_output_aliases={2: 0},
        compiler_params=_params(("arbitrary",)),
        name="moe_scatter",
    )(slot_tiles, h2p, xs0)


def _expert_kernel(be_ref, bi_ref, nb_ref, xs_ref, wg_ref, bg_ref, wu_ref, bu_ref, wd_ref, bd_ref, yb_ref,
                   wg_scr, wu_scr, wd_scr):
    del bi_ref
    half = xs_ref.shape[1]
    i = pl.program_id(0)

    @pl.when(jnp.logical_or(i == 0, be_ref[i] != be_ref[jnp.maximum(i - 1, 0)]))
    def _():
        wg_scr[...] = wg_ref[0].astype(BF16)
        wu_scr[...] = wu_ref[0].astype(BF16)
        wd_scr[...] = wd_ref[0].astype(BF16)

    @pl.when(i < nb_ref[0])
    def _():
        xp = xs_ref[...]
        xlo = _unpack_lo(xp).astype(BF16)
        xhi = _unpack_hi(xp).astype(BF16)
        gt = _dot(xlo, wg_scr[:half, :]) + _dot(xhi, wg_scr[half:, :]) + bg_ref[0]
        up = _dot(xlo, wu_scr[:half, :]) + _dot(xhi, wu_scr[half:, :]) + bu_ref[0]
        gt = jnp.minimum(gt, SWIGLU_LIMIT)
        up = jnp.clip(up, -SWIGLU_LIMIT, SWIGLU_LIMIT)
        act = (up + 1.0) * (gt * _sigmoid(SWIGLU_ALPHA * gt))
        y = _dot(act.astype(BF16), wd_scr[...]) + bd_ref[0]
        yb_ref[...] = _pack_halves(y)

    @pl.when(pl.program_id(0) >= nb_ref[0])
    def _():
        yb_ref[...] = jnp.zeros_like(yb_ref)


def _experts(layer, block_e, block_i, n_used, xs, wg, bg, wu, bu, wd, bd):
    n_slots, half = xs.shape
    r = EXPERT_R
    depth, ne, d, dff = wg.shape
    wmap = lambda i, be, bi, nb: (layer, be[i], 0, 0)
    xmap = lambda i, be, bi, nb: (bi[i], 0)
    grid_spec = pltpu.PrefetchScalarGridSpec(
        num_scalar_prefetch=3,
        grid=(n_slots // r,),
        in_specs=[pl.BlockSpec((r, half), xmap),
                  pl.BlockSpec((None, 1, d, dff), wmap),
                  pl.BlockSpec((None, 1, 1, dff), wmap),
                  pl.BlockSpec((None, 1, d, dff), wmap),
                  pl.BlockSpec((None, 1, 1, dff), wmap),
                  pl.BlockSpec((None, 1, dff, d), wmap),
                  pl.BlockSpec((None, 1, 1, d), wmap)],
        out_specs=pl.BlockSpec((r, half), lambda i, be, bi, nb: (i, 0)),
        scratch_shapes=[pltpu.VMEM((d, dff), BF16), pltpu.VMEM((d, dff), BF16), pltpu.VMEM((dff, d), BF16)],
    )
    return pl.pallas_call(
        _expert_kernel,
        grid_spec=grid_spec,
        out_shape=jax.ShapeDtypeStruct((n_slots, half), U32),
        compiler_params=_params(("arbitrary",)),
        name="moe_experts",
    )(block_e, block_i, n_used, xs, wg, bg.reshape(depth, ne, 1, dff), wu, bu.reshape(depth, ne, 1, dff),
      wd, bd.reshape(depth, ne, 1, d))


def _combine_kernel(slot_hbm, yb_hbm, rp_ref, x1_ref, g2_ref, fg_ref, out_ref, buf, idx_smem, isem, sem, *, final):
    tm = x1_ref.shape[1]
    half = buf.shape[2]
    i = pl.program_id(0) * pl.num_programs(1) + pl.program_id(1)
    cp = pltpu.make_async_copy(slot_hbm.at[i], idx_smem, isem)
    cp.start()
    cp.wait()

    def body(r, carry):
        for kk in range(TOP_K):
            src = idx_smem[r * TOP_K + kk]
            pltpu.make_async_copy(yb_hbm.at[pl.ds(src, 1)], buf.at[kk, pl.ds(r, 1)], sem).start()
        return carry

    lax.fori_loop(0, tm, body, 0, unroll=8)
    for kk in range(TOP_K):
        _row_copy_wait(yb_hbm.at[pl.ds(0, tm)], buf.at[kk], sem)

    rp = rp_ref[0]
    ylo = jnp.zeros((tm, half), F32)
    yhi = jnp.zeros((tm, half), F32)
    for kk in range(TOP_K):
        w = buf[kk]
        pk = rp[:, kk:kk + 1]
        ylo = ylo + pk * _unpack_lo(w)
        yhi = yhi + pk * _unpack_hi(w)
    x2 = x1_ref[0] + g2_ref[0] * jnp.concatenate([ylo, yhi], axis=1)
    if final:
        x2 = x2 * lax.rsqrt(jnp.mean(x2 * x2, axis=-1, keepdims=True) + NORM_EPS) * fg_ref[...]
    out_ref[0] = x2


def _combine(slot_tiles, yb, rp, x1, g2, fg, final):
    b, s, d = x1.shape
    tm = slot_tiles.shape[1] // TOP_K
    row = lambda i, j: (i, j, 0)
    return pl.pallas_call(
        functools.partial(_combine_kernel, final=final),
        grid=(b, s // tm),
        in_specs=[pl.BlockSpec(memory_space=pl.ANY),
                  pl.BlockSpec(memory_space=pl.ANY),
                  pl.BlockSpec((1, tm, LANES), row),
                  pl.BlockSpec((1, tm, d), row),
                  pl.BlockSpec((1, 1, d), lambda i, j: (i, 0, 0)),
                  pl.BlockSpec((1, d), lambda i, j: (0, 0))],
        out_specs=pl.BlockSpec((1, tm, d), row),
        out_shape=jax.ShapeDtypeStruct((b, s, d), F32),
        scratch_shapes=[pltpu.VMEM((TOP_K, tm, d // 2), U32),
                        pltpu.SMEM((tm * TOP_K,), I32),
                        pltpu.SemaphoreType.DMA(()),
                        pltpu.SemaphoreType.DMA(())],
        compiler_params=_params(("arbitrary", "arbitrary")),
        name="moe_combine",
    )(slot_tiles, yb, rp, x1, g2, fg)


def _permute_w_in(w_in):
    d = w_in.shape[0]
    g0 = 4 * M_WIDTH
    a0 = g0 + 2 * M_HEADS
    pad = jnp.zeros((d, LANES - 2 * M_HEADS), w_in.dtype)
    return jnp.concatenate([w_in[:, :g0], w_in[:, a0:], w_in[:, g0:a0], pad], axis=1).astype(BF16)


def kernel(x, c, positions, w_ada, b_ada, norm1_g, w_in, conv_w, conv_b, b_igate, b_fgate, mlstm_norm_g,
           w_out, norm2_g, w_router, b_router, w_gate, b_gate, w_up, b_up, w_down, b_down, final_norm_g):
    b, s, d = x.shape
    depth = w_ada.shape[0]
    t = b * s
    tm = min(MOE_TM, s)
    n_slots = (t * TOP_K // EXPERT_R + N_EXPERTS) * EXPERT_R
    n_blocks = n_slots // EXPERT_R

    mod = _modulation(c, w_ada, b_ada)
    cos_t, sin_t = _rope_tables(positions)

    for l in range(depth):
        sh1, sc1, g1, sh2, sc2, g2 = [mod[l, :, i] for i in range(6)]
        gate_b = jnp.zeros((1, LANES), F32).at[0, :M_HEADS].set(b_igate[l]).at[0, M_HEADS:2 * M_HEADS].set(b_fgate[l])
        zqk, zv, zo, za, vt, zg, kmean = _in_projection(
            x, norm1_g[l].reshape(1, d), sc1, sh1, _permute_w_in(w_in[l]), conv_w[l],
            conv_b[l].reshape(1, -1), gate_b, cos_t, sin_t)
        hm = _mlstm(zqk, zv, zo, zg, mlstm_norm_g[l].reshape(1, -1))
        ha = _moba(za, vt, kmean.reshape(b, s // MOBA_BLOCK, A_WIDTH))

        wr = jnp.zeros((d, LANES), F32).at[:, :N_EXPERTS].set(w_router[l])
        wr_hi = wr.astype(BF16)
        wr_lo = (wr - wr_hi.astype(F32)).astype(BF16)
        b_r = jnp.full((1, LANES), NEG_BIG, F32).at[0, :N_EXPERTS].set(b_router[l])
        x1, h2p, ri, rp, cnt = _out_projection(
            hm, ha, w_out[l].astype(BF16), x, g1, norm2_g[l].reshape(1, d), sc2, sh2, wr_hi, wr_lo, b_r)

        counts = cnt[0, :N_EXPERTS].astype(I32)
        nblk = (counts + EXPERT_R - 1) // EXPERT_R
        blk_end = jnp.cumsum(nblk)
        blk_start = blk_end - nblk
        n_used = blk_end[-1:]
        block_i = jnp.minimum(jnp.arange(n_blocks, dtype=I32), n_used[0] - 1)
        block_e = jnp.minimum(jnp.sum((blk_end[None, :] <= block_i[:, None]).astype(I32), axis=1), N_EXPERTS - 1)
        eid = ri[..., :TOP_K].reshape(t, TOP_K)
        rank = ri[..., TOP_K:2 * TOP_K].reshape(t, TOP_K)
        slot = blk_start[eid] * EXPERT_R + rank
        slot_tiles = slot.reshape(t // tm, tm * TOP_K)

        xs = _scatter_rows(slot_tiles, h2p.reshape(t, d // 2), n_slots)
        yb = _experts(l, block_e, block_i, n_used.astype(I32), xs, w_gate, b_gate, w_up, b_up, w_down, b_down)
        x = _combine(slot_tiles, yb, rp, x1, g2, final_norm_g.reshape(1, d), final=(l == depth - 1))
    return x
```

```python
import functools

import jax
import jax.numpy as jnp
from jax import lax
from jax.experimental import pallas as pl
from jax.experimental.pallas import tpu as pltpu

F32 = jnp.float32
BF16 = jnp.bfloat16
I32 = jnp.int32

M_HEADS = 4
M_HEAD_DIM = 128
M_WIDTH = M_HEADS * M_HEAD_DIM
A_HEADS = 8
A_HEAD_DIM = 64
A_WIDTH = A_HEADS * A_HEAD_DIM
CONV_K = 4
GATE_SOFTCAP = 15.0
MOBA_BLOCK = 256
MOBA_TOPK = 3
ROPE_THETA = 500000.0
ROPE_DIM = A_HEAD_DIM // 4
N_EXPERTS = 32
TOP_K = 4
SWIGLU_LIMIT = 7.0
SWIGLU_ALPHA = 1.702
NORM_EPS = 1e-5

LANES = 128
SUBLANES = 8
VMEM_LIMIT = 56 * 1024 * 1024

NEG_BIG = -1e30
MLSTM_L = 256
IN_TM = 256
OUT_TM = 512
MOE_TM = 512
EXPERT_R = 512
VT_ROWS = A_HEAD_DIM + 16
LOG2E = 1.4426950408889634

C_QK = 0
C_V = 2 * M_WIDTH
C_O = 3 * M_WIDTH
C_A = 4 * M_WIDTH
C_G = 4 * M_WIDTH + 3 * A_WIDTH
N_COLS = C_G + LANES


def _dot(a, b):
    return jnp.dot(a, b, preferred_element_type=F32)


def _dot_nt(a, b):
    return lax.dot_general(a, b, (((1,), (1,)), ((), ())), preferred_element_type=F32)


def _split(a):
    hi = a.astype(BF16)
    lo = (a - hi.astype(F32)).astype(BF16)
    return hi, lo


def _dot3(a, b):
    ah, al = _split(a)
    bh, bl = _split(b)
    return _dot(ah, bh) + (_dot(al, bh) + _dot(ah, bl))


def _sigmoid(t):
    return 1.0 / (1.0 + jnp.exp(-t))


def _params(sem):
    return pltpu.CompilerParams(dimension_semantics=sem, vmem_limit_bytes=VMEM_LIMIT)


def _mod_kernel(c_ref, w_ref, b_ref, o_ref):
    c = c_ref[...]
    cond = c * _sigmoid(c)
    o_ref[0] = _dot3(cond, w_ref[0]) + b_ref[0]


def _modulation(c, w_ada, b_ada):
    depth, d, n6 = w_ada.shape
    b = c.shape[0]
    rows = -(-b // SUBLANES) * SUBLANES
    cp = jnp.zeros((rows, d), F32).at[:b].set(c)
    tn = 1536
    out = pl.pallas_call(
        _mod_kernel,
        grid=(depth, n6 // tn),
        in_specs=[pl.BlockSpec((rows, d), lambda l, j: (0, 0)),
                  pl.BlockSpec((1, d, tn), lambda l, j: (l, 0, j)),
                  pl.BlockSpec((1, 1, tn), lambda l, j: (l, 0, j))],
        out_specs=pl.BlockSpec((1, rows, tn), lambda l, j: (l, 0, j)),
        out_shape=jax.ShapeDtypeStruct((depth, rows, n6), F32),
        compiler_params=_params(("arbitrary", "arbitrary")),
        name="adaln_mod",
    )(cp, w_ada, b_ada.reshape(depth, 1, n6))
    return out[:, :b].reshape(depth, b, 6, 1, d)


def _rope_kernel(pos_ref, invf_ref, sign_ref, cos_ref, sin_ref):
    ang = pos_ref[0].astype(F32) * invf_ref[...]
    cos_ref[0] = jnp.cos(ang)
    sin_ref[0] = jnp.sin(ang) * sign_ref[...]


def _rope_tables(positions):
    b, s = positions.shape
    half = ROPE_DIM // 2
    inv_freq = ROPE_THETA ** (-jnp.arange(0, ROPE_DIM, 2, dtype=F32) / ROPE_DIM)
    lane = jnp.arange(LANES) % A_HEAD_DIM
    invf = jnp.where(lane < ROPE_DIM, inv_freq[lane % half], 0.0).astype(F32).reshape(1, LANES)
    sign = jnp.where(lane < half, -1.0, jnp.where(lane < ROPE_DIM, 1.0, 0.0)).astype(F32).reshape(1, LANES)
    ts = min(s, 1024)
    tab = jax.ShapeDtypeStruct((b, s, LANES), F32)
    return pl.pallas_call(
        _rope_kernel,
        grid=(b, s // ts),
        in_specs=[pl.BlockSpec((1, ts, 1), lambda i, j: (i, j, 0)),
                  pl.BlockSpec((1, LANES), lambda i, j: (0, 0)),
                  pl.BlockSpec((1, LANES), lambda i, j: (0, 0))],
        out_specs=[pl.BlockSpec((1, ts, LANES), lambda i, j: (i, j, 0))] * 2,
        out_shape=[tab, tab],
        compiler_params=_params(("arbitrary", "arbitrary")),
        name="rope_tables",
    )(positions.reshape(b, s, 1), invf, sign)


def _inproj_kernel(x_ref, g_ref, sc_ref, sh_ref, w_ref, cw_ref, cb_ref, gb_ref, cos_ref, sin_ref,
                   zqk_ref, zv_ref, zo_ref, za_ref, vt_ref, zg_ref, km_ref, halo_ref):
    tm = x_ref.shape[1]
    x = x_ref[0]
    var = jnp.mean(x * x, axis=-1, keepdims=True)
    h = x * lax.rsqrt(var + NORM_EPS) * g_ref[...]
    h = h * (1.0 + sc_ref[0]) + sh_ref[0]
    hb = h.astype(BF16)

    @pl.when(pl.program_id(1) == 0)
    def _():
        halo_ref[...] = jnp.zeros_like(halo_ref)

    z = _dot(hb, w_ref[:, C_QK:C_V])
    prev = halo_ref[...]
    row8 = lax.broadcasted_iota(I32, prev.shape, 0)
    acc = z * cw_ref[CONV_K - 1:CONV_K, :] + cb_ref[...]
    for k in range(1, CONV_K):
        zs = pltpu.roll(z, k, 0)
        ps = pltpu.roll(prev, k, 0)
        head = jnp.where(row8 < k, ps, zs[:SUBLANES])
        zs = jnp.concatenate([head, zs[SUBLANES:]], axis=0)
        acc = acc + zs * cw_ref[CONV_K - 1 - k:CONV_K - k, :]
    halo_ref[...] = z[tm - SUBLANES:, :]
    qk = acc * _sigmoid(acc)
    col = lax.broadcasted_iota(I32, (1, 2 * M_WIDTH), 1)
    kscale = jnp.where(col < M_WIDTH, 1.0, M_HEAD_DIM ** -0.5).astype(F32)
    zqk_ref[0] = (qk * kscale).astype(BF16)

    zv_ref[0] = _dot(hb, w_ref[:, C_V:C_O]).astype(BF16)
    zo_ref[0] = _dot(hb, w_ref[:, C_O:C_A]).astype(BF16)

    za = _dot(hb, w_ref[:, C_A:C_G])
    cos = cos_ref[0]
    sin = sin_ref[0]
    lane = lax.broadcasted_iota(I32, (tm, LANES), 1)
    first_half = (lane % A_HEAD_DIM) < (ROPE_DIM // 2)
    for c in range(2 * A_WIDTH // LANES):
        xc = za[:, c * LANES:(c + 1) * LANES]
        partner = jnp.where(first_half, pltpu.roll(xc, LANES - ROPE_DIM // 2, 1), pltpu.roll(xc, ROPE_DIM // 2, 1))
        rc = xc * cos + partner * sin
        za_ref[0, :, c * LANES:(c + 1) * LANES] = rc.astype(BF16)
        if c >= A_WIDTH // LANES:
            kc = c - A_WIDTH // LANES
            km_ref[0, 0, :, kc * LANES:(kc + 1) * LANES] = jnp.mean(rc, axis=0, keepdims=True)
    for c in range(A_WIDTH // LANES):
        vc = za[:, 2 * A_WIDTH + c * LANES:2 * A_WIDTH + (c + 1) * LANES]
        vct = vc.T
        extra = (lax.broadcasted_iota(I32, (VT_ROWS - A_HEAD_DIM, tm), 0) == 0).astype(F32)
        for hd in range(LANES // A_HEAD_DIM):
            head_t = vct[hd * A_HEAD_DIM:(hd + 1) * A_HEAD_DIM, :]
            vt_ref[0, c * (LANES // A_HEAD_DIM) + hd, 0] = jnp.concatenate([head_t, extra], axis=0).astype(BF16)

    t = _dot(hb, w_ref[:, C_G:N_COLS]) + gb_ref[...]
    t = GATE_SOFTCAP * jnp.tanh(t / GATE_SOFTCAP)
    logsig = jnp.minimum(t, 0.0) - jnp.log(1.0 + jnp.exp(-jnp.abs(t)))
    zg_ref[0] = jnp.where(lane < M_HEADS, t, logsig)


def _in_projection(x, g, sc, sh, w_perm, conv_w, conv_b, gate_b, cos_t, sin_t):
    b, s, d = x.shape
    tm = IN_TM
    assert tm == MOBA_BLOCK and s % tm == 0
    nt = s // tm
    row = lambda i, j: (i, j, 0)
    bcast = lambda i, j: (i, 0, 0)
    const = lambda i, j: (0, 0)
    out_shape = [
        jax.ShapeDtypeStruct((b, s, 2 * M_WIDTH), BF16),
        jax.ShapeDtypeStruct((b, s, M_WIDTH), BF16),
        jax.ShapeDtypeStruct((b, s, M_WIDTH), BF16),
        jax.ShapeDtypeStruct((b, s, 2 * A_WIDTH), BF16),
        jax.ShapeDtypeStruct((b, A_HEADS, nt, VT_ROWS, tm), BF16),
        jax.ShapeDtypeStruct((b, s, LANES), F32),
        jax.ShapeDtypeStruct((b, nt, 1, A_WIDTH), F32),
    ]
    out_specs = [
        pl.BlockSpec((1, tm, 2 * M_WIDTH), row),
        pl.BlockSpec((1, tm, M_WIDTH), row),
        pl.BlockSpec((1, tm, M_WIDTH), row),
        pl.BlockSpec((1, tm, 2 * A_WIDTH), row),
        pl.BlockSpec((1, A_HEADS, 1, VT_ROWS, tm), lambda i, j: (i, 0, j, 0, 0)),
        pl.BlockSpec((1, tm, LANES), row),
        pl.BlockSpec((1, 1, 1, A_WIDTH), lambda i, j: (i, j, 0, 0)),
    ]
    in_specs = [
        pl.BlockSpec((1, tm, d), row),
        pl.BlockSpec((1, d), const),
        pl.BlockSpec((1, 1, d), bcast),
        pl.BlockSpec((1, 1, d), bcast),
        pl.BlockSpec((d, N_COLS), const),
        pl.BlockSpec((CONV_K, 2 * M_WIDTH), const),
        pl.BlockSpec((1, 2 * M_WIDTH), const),
        pl.BlockSpec((1, LANES), const),
        pl.BlockSpec((1, tm, LANES), row),
        pl.BlockSpec((1, tm, LANES), row),
    ]
    return pl.pallas_call(
        _inproj_kernel,
        grid=(b, nt),
        in_specs=in_specs,
        out_specs=out_specs,
        out_shape=out_shape,
        scratch_shapes=[pltpu.VMEM((SUBLANES, 2 * M_WIDTH), F32)],
        compiler_params=_params(("arbitrary", "arbitrary")),
        name="in_proj",
    )(x, g, sc, sh, w_perm, conv_w, conv_b, gate_b, cos_t, sin_t)


def _mlstm_kernel(qk_ref, v_ref, o_ref, g_ref, ng_ref, out_ref, c_scr, n_scr, m_scr):
    L = qk_ref.shape[1]

    @pl.when(pl.program_id(1) == 0)
    def _():
        c_scr[...] = jnp.zeros_like(c_scr)
        n_scr[...] = jnp.zeros_like(n_scr)
        m_scr[...] = jnp.full_like(m_scr, -jnp.inf)

    gates = g_ref[0]
    r = lax.broadcasted_iota(I32, (L, L), 0)
    c = lax.broadcasted_iota(I32, (L, L), 1)
    causal = c <= r
    tri = causal.astype(BF16)
    gh, gl = _split(gates)
    cum = _dot(tri, gh) + _dot(tri, gl)
    lane = lax.broadcasted_iota(I32, gates.shape, 1)
    mix = jnp.where(lane < M_HEADS, gates, cum)
    mix_t = mix.T

    for hd in range(M_HEADS):
        q = qk_ref[0, :, hd * M_HEAD_DIM:(hd + 1) * M_HEAD_DIM]
        k = qk_ref[0, :, M_WIDTH + hd * M_HEAD_DIM:M_WIDTH + (hd + 1) * M_HEAD_DIM]
        v = v_ref[0, :, hd * M_HEAD_DIM:(hd + 1) * M_HEAD_DIM]
        ig_col = mix[:, hd:hd + 1]
        b_col = mix[:, M_HEADS + hd:M_HEADS + hd + 1]
        ig_row = mix_t[hd:hd + 1, :]
        b_row = mix_t[M_HEADS + hd:M_HEADS + hd + 1, :]
        g_tot = b_row[:, L - 1:L]
        m_prev = m_scr[hd, 0:1, 0:1]
        c_prev = c_scr[hd]
        n_prev = n_scr[hd, 0:1, :]

        a_col = b_col + m_prev
        dmat = jnp.where(causal, b_col - b_row + ig_row, -jnp.inf)
        m_t = jnp.maximum(a_col, jnp.max(dmat, axis=1, keepdims=True))
        s = _dot_nt(q, k) * jnp.exp(dmat - m_t)
        inter = jnp.exp(a_col - m_t)
        num = _dot(s.astype(BF16), v) + inter * _dot(q, c_prev.astype(BF16))
        qn = jnp.sum(q.astype(F32) * n_prev, axis=1, keepdims=True)
        den = jnp.sum(s, axis=1, keepdims=True) + inter * qn
        hh = num / jnp.maximum(jnp.abs(den), jnp.exp(-m_t))

        w_row = g_tot - b_row + ig_row
        m_new = jnp.maximum(g_tot + m_prev, jnp.max(w_row, axis=1, keepdims=True))
        decay = jnp.exp(g_tot + m_prev - m_new)
        kw = k.astype(F32) * jnp.exp(g_tot - b_col + ig_col - m_new)
        c_scr[hd] = decay * c_prev + _dot(kw.T.astype(BF16), v)
        n_scr[hd] = jnp.broadcast_to(decay * n_prev + jnp.sum(kw, axis=0, keepdims=True), n_scr.shape[1:])
        m_scr[hd] = jnp.broadcast_to(m_new, m_scr.shape[1:])

        og = _sigmoid(o_ref[0, :, hd * M_HEAD_DIM:(hd + 1) * M_HEAD_DIM].astype(F32))
        ho = hh * og
        hn = ho * lax.rsqrt(jnp.mean(ho * ho, axis=-1, keepdims=True) + NORM_EPS)
        out_ref[0, :, hd * M_HEAD_DIM:(hd + 1) * M_HEAD_DIM] = (
            hn * ng_ref[:, hd * M_HEAD_DIM:(hd + 1) * M_HEAD_DIM]).astype(BF16)


def _mlstm(zqk, zv, zo, zg, norm_g):
    b, s, _ = zqk.shape
    L = min(MLSTM_L, s)
    row = lambda i, j: (i, j, 0)
    return pl.pallas_call(
        _mlstm_kernel,
        grid=(b, s // L),
        in_specs=[pl.BlockSpec((1, L, 2 * M_WIDTH), row),
                  pl.BlockSpec((1, L, M_WIDTH), row),
                  pl.BlockSpec((1, L, M_WIDTH), row),
                  pl.BlockSpec((1, L, LANES), row),
                  pl.BlockSpec((1, M_WIDTH), lambda i, j: (0, 0))],
        out_specs=pl.BlockSpec((1, L, M_WIDTH), row),
        out_shape=jax.ShapeDtypeStruct((b, s, M_WIDTH), BF16),
        scratch_shapes=[pltpu.VMEM((M_HEADS, M_HEAD_DIM, M_HEAD_DIM), F32),
                        pltpu.VMEM((M_HEADS, SUBLANES, M_HEAD_DIM), F32),
                        pltpu.VMEM((M_HEADS, SUBLANES, LANES), F32)],
        compiler_params=_params(("arbitrary", "arbitrary")),
        name="mlstm",
    )(zqk, zv, zo, zg, norm_g)


def _moba_kernel(q_ref, k_ref, vt_ref, km_ref, o_ref, qs_scr, kill_scr, acc_scr, st_scr, st2_scr):
    tq = q_ref.shape[1]
    nb = km_ref.shape[1]
    own = pl.program_id(1)
    pairs = q_ref.shape[2] // LANES
    per = LANES // A_HEAD_DIM
    chains = [(hp, hd) for hp in range(pairs) for hd in range(per)]

    lane = lax.broadcasted_iota(I32, (tq, LANES), 1)
    head0 = lane < A_HEAD_DIM
    blk = lax.broadcasted_iota(I32, (nb, per * tq), 0)
    for hp in range(pairs):
        q = q_ref[0, :, hp * LANES:(hp + 1) * LANES]
        zero = jnp.zeros_like(q)
        q2 = jnp.concatenate([jnp.where(head0, q, zero), jnp.where(head0, zero, q)], axis=0)
        qs_scr[hp] = (q2.astype(F32) * (A_HEAD_DIM ** -0.5 * LOG2E)).astype(BF16)

        kmh, kml = _split(km_ref[0, :, hp * LANES:(hp + 1) * LANES])
        gate = _dot_nt(kmh, q2) + _dot_nt(kml, q2)
        g0 = jnp.where(blk < own, gate, -jnp.inf)
        sel = jnp.zeros(gate.shape, jnp.bool_)
        for _ in range(MOBA_TOPK):
            mx = jnp.max(g0, axis=0, keepdims=True)
            hit = jnp.logical_and(g0 == mx, mx > -jnp.inf)
            idx = jnp.min(jnp.where(hit, blk, nb), axis=0, keepdims=True)
            pick = blk == idx
            sel = jnp.logical_or(sel, pick)
            g0 = jnp.where(pick, -jnp.inf, g0)
        kill_scr[:, hp * per * tq:(hp + 1) * per * tq] = jnp.where(sel, 0.0, -NEG_BIG).astype(F32)

    def scores(kj, hp, hd):
        return _dot_nt(kj[:, hp * LANES:(hp + 1) * LANES], qs_scr[hp, hd * tq:(hd + 1) * tq, :])

    def values(j, c, p):
        return _dot(vt_ref[0, c, j], p.astype(BF16))

    off = pl.multiple_of(own * tq, tq)
    kj = k_ref[0, pl.ds(off, tq), :]
    key = lax.broadcasted_iota(I32, (tq, tq), 0)
    qry = lax.broadcasted_iota(I32, (tq, tq), 1)
    sts = [jnp.where(key <= qry, scores(kj, hp, hd), NEG_BIG) for hp, hd in chains]
    state = []
    for c in range(len(chains)):
        m = jnp.max(sts[c], axis=0, keepdims=True)
        state.append(m)
        acc_scr[c] = values(own, c, jnp.exp2(sts[c] - m))

    def stage_scores(j, dst):
        offj = pl.multiple_of(j * tq, tq)
        kj = k_ref[0, pl.ds(offj, tq), :]
        for c, (hp, hd) in enumerate(chains):
            dst[c] = scores(kj, hp, hd)

    def step(j, src, dst, carry):
        stage_scores(j + 1, dst)
        kill = kill_scr[pl.ds(j, 1), :]
        acc_prev = [acc_scr[c] for c in range(len(chains))]
        out, alphas, pvs = [], [], []
        for c in range(len(chains)):
            m = carry[c]
            kh = kill[:, c * tq:(c + 1) * tq]
            mn = jnp.maximum(m, jnp.max(src[c], axis=0, keepdims=True) - kh)
            out.append(mn)
            alphas.append(jnp.exp2(m - mn))
            pvs.append(values(j, c, jnp.exp2(src[c] - (mn + kh))))
        for c in range(len(chains)):
            acc_scr[c] = alphas[c] * acc_prev[c] + pvs[c]
        return tuple(out)

    stage_scores(0, st_scr)

    def body(i, carry):
        carry = step(2 * i, st_scr, st2_scr, carry)
        return step(2 * i + 1, st2_scr, st_scr, carry)

    state = lax.fori_loop(0, own // 2, body, tuple(state))
    lax.cond(own % 2 == 1, lambda st: step(own - 1, st_scr, st2_scr, st), lambda st: st, state)
    for hp in range(pairs):
        heads_t = []
        for hd in range(per):
            acc = acc_scr[hp * per + hd]
            heads_t.append(acc[:A_HEAD_DIM] / acc[A_HEAD_DIM:A_HEAD_DIM + 1])
        o_ref[0, :, hp * LANES:(hp + 1) * LANES] = jnp.concatenate(heads_t, axis=0).T.astype(BF16)


def _moba(za, vt, kmean):
    b, s, _ = za.shape
    tq = MOBA_BLOCK
    nb = s // tq
    pairs = A_WIDTH // LANES
    once = pl.Buffered(1)
    return pl.pallas_call(
        _moba_kernel,
        grid=(b, nb),
        in_specs=[pl.BlockSpec((1, tq, A_WIDTH), lambda i, j: (i, j, 0)),
                  pl.BlockSpec((1, s, A_WIDTH), lambda i, j: (i, 0, 1), pipeline_mode=once),
                  pl.BlockSpec((1, A_HEADS, nb, VT_ROWS, tq), lambda i, j: (i, 0, 0, 0, 0), pipeline_mode=once),
                  pl.BlockSpec((1, nb, A_WIDTH), lambda i, j: (i, 0, 0))],
        out_specs=pl.BlockSpec((1, tq, A_WIDTH), lambda i, j: (i, j, 0)),
        out_shape=jax.ShapeDtypeStruct((b, s, A_WIDTH), BF16),
        scratch_shapes=[pltpu.VMEM((pairs, 2 * tq, LANES), BF16),
                        pltpu.VMEM((nb, 2 * pairs * tq), F32),
                        pltpu.VMEM((A_HEADS, VT_ROWS, tq), F32),
                        pltpu.VMEM((A_HEADS, tq, tq), F32),
                        pltpu.VMEM((A_HEADS, tq, tq), F32)],
        compiler_params=_params(("arbitrary", "arbitrary")),
        name="moba",
    )(za, za, vt, kmean)


def _outproj_kernel(hm_ref, ha_ref, wo_ref, x_ref, g1_ref, n2_ref, sc_ref, sh_ref, wrh_ref, wrl_ref, br_ref,
                    x1_ref, h2p_ref, ri_ref, rp_ref, cnt_ref, cnt_scr):
    tm = x_ref.shape[1]

    @pl.when(jnp.logical_and(pl.program_id(0) == 0, pl.program_id(1) == 0))
    def _():
        cnt_scr[...] = jnp.zeros_like(cnt_scr)

    y = _dot(hm_ref[0], wo_ref[:M_WIDTH, :]) + _dot(ha_ref[0], wo_ref[M_WIDTH:, :])
    x1 = x_ref[0] + g1_ref[0] * y
    x1_ref[0] = x1
    var = jnp.mean(x1 * x1, axis=-1, keepdims=True)
    h2 = x1 * lax.rsqrt(var + NORM_EPS) * n2_ref[...]
    h2 = h2 * (1.0 + sc_ref[0]) + sh_ref[0]
    h2p_ref[0] = h2

    hh, hl = _split(h2)
    logits = _dot(hh, wrh_ref[...]) + (_dot(hl, wrh_ref[...]) + _dot(hh, wrl_ref[...])) + br_ref[...]
    lane = lax.broadcasted_iota(I32, logits.shape, 1)
    g0 = logits
    vals, idxs = [], []
    for _ in range(TOP_K):
        mx = jnp.max(g0, axis=1, keepdims=True)
        idx = jnp.min(jnp.where(g0 == mx, lane, LANES), axis=1, keepdims=True)
        vals.append(mx)
        idxs.append(idx)
        g0 = jnp.where(lane == idx, -jnp.inf, g0)
    ex = [jnp.exp(vv - vals[0]) for vv in vals]
    tot = ex[0] + ex[1] + ex[2] + ex[3]

    onehot = jnp.zeros(logits.shape, F32)
    for idx in idxs:
        onehot = onehot + (lane == idx).astype(F32)
    r = lax.broadcasted_iota(I32, (tm, tm), 0)
    c = lax.broadcasted_iota(I32, (tm, tm), 1)
    before = _dot((c < r).astype(BF16), onehot.astype(BF16)) + cnt_scr[0:1, :]
    ri = jnp.zeros(logits.shape, I32)
    rp = jnp.zeros(logits.shape, F32)
    for kk in range(TOP_K):
        rank = jnp.sum(jnp.where(lane == idxs[kk], before, 0.0), axis=1, keepdims=True)
        ri = jnp.where(lane == kk, idxs[kk], ri)
        ri = jnp.where(lane == TOP_K + kk, rank.astype(I32), ri)
        rp = jnp.where(lane == kk, ex[kk] / tot, rp)
    ri_ref[0] = ri
    rp_ref[0] = rp
    cnt = cnt_scr[...] + jnp.sum(onehot, axis=0, keepdims=True)
    cnt_scr[...] = cnt
    cnt_ref[...] = cnt


def _out_projection(hm, ha, w_out, x, g1, n2g, sc2, sh2, wr_hi, wr_lo, b_r):
    b, s, d = x.shape
    tm = min(OUT_TM, s)
    row = lambda i, j: (i, j, 0)
    bcast = lambda i, j: (i, 0, 0)
    const = lambda i, j: (0, 0)
    out_shape = [
        jax.ShapeDtypeStruct((b, s, d), F32),
        jax.ShapeDtypeStruct((b, s, d), F32),
        jax.ShapeDtypeStruct((b, s, LANES), I32),
        jax.ShapeDtypeStruct((b, s, LANES), F32),
        jax.ShapeDtypeStruct((SUBLANES, LANES), F32),
    ]
    out_specs = [
        pl.BlockSpec((1, tm, d), row),
        pl.BlockSpec((1, tm, d), row),
        pl.BlockSpec((1, tm, LANES), row),
        pl.BlockSpec((1, tm, LANES), row),
        pl.BlockSpec((SUBLANES, LANES), const),
    ]
    in_specs = [
        pl.BlockSpec((1, tm, M_WIDTH), row),
        pl.BlockSpec((1, tm, A_WIDTH), row),
        pl.BlockSpec((M_WIDTH + A_WIDTH, d), const),
        pl.BlockSpec((1, tm, d), row),
        pl.BlockSpec((1, 1, d), bcast),
        pl.BlockSpec((1, d), const),
        pl.BlockSpec((1, 1, d), bcast),
        pl.BlockSpec((1, 1, d), bcast),
        pl.BlockSpec((d, LANES), const),
        pl.BlockSpec((d, LANES), const),
        pl.BlockSpec((1, LANES), const),
    ]
    return pl.pallas_call(
        _outproj_kernel,
        grid=(b, s // tm),
        in_specs=in_specs,
        out_specs=out_specs,
        out_shape=out_shape,
        scratch_shapes=[pltpu.VMEM((SUBLANES, LANES), F32)],
        compiler_params=_params(("arbitrary", "arbitrary")),
        name="out_proj_router",
    )(hm, ha, w_out, x, g1, n2g, sc2, sh2, wr_hi, wr_lo, b_r)


def _row_copy_wait(src, dst, sem):
    pltpu.make_async_copy(src, dst, sem).wait()


def _scatter_kernel(slot_hbm, h2p_ref, xs_in, xs_out, idx_smem, isem, sem):
    del xs_in
    tm = h2p_ref.shape[0]
    i = pl.program_id(0)
    cp = pltpu.make_async_copy(slot_hbm.at[i], idx_smem, isem)
    cp.start()
    cp.wait()

    def body(g, carry):
        base = pl.multiple_of(g * SUBLANES, SUBLANES)
        for rr in range(SUBLANES):
            for kk in range(TOP_K):
                dst = idx_smem[(base + rr) * TOP_K + kk]
                pltpu.make_async_copy(h2p_ref.at[pl.ds(base + rr, 1)], xs_out.at[pl.ds(dst, 1)],
                                      sem).start(priority=kk % 2)
        return carry

    lax.fori_loop(0, tm // SUBLANES, body, 0)
    for kk in range(TOP_K):
        _row_copy_wait(h2p_ref, xs_out.at[pl.ds(0, tm)], sem)


def _scatter_rows(slot_tiles, h2p, n_slots):
    t, w = h2p.shape
    tm = slot_tiles.shape[1] // TOP_K
    xs0 = jnp.zeros((n_slots, w), h2p.dtype)
    return pl.pallas_call(
        _scatter_kernel,
        grid=(t // tm,),
        in_specs=[pl.BlockSpec(memory_space=pl.ANY),
                  pl.BlockSpec((tm, w), lambda i: (i, 0)),
                  pl.BlockSpec(memory_space=pl.ANY)],
        out_specs=pl.BlockSpec(memory_space=pl.ANY),
        out_shape=jax.ShapeDtypeStruct((n_slots, w), h2p.dtype),
        scratch_shapes=[pltpu.SMEM((tm * TOP_K,), I32),
                        pltpu.SemaphoreType.DMA(()),
                        pltpu.SemaphoreType.DMA(())],
        input_output_aliases={2: 0},
        compiler_params=_params(("arbitrary",)),
        name="moe_scatter",
    )(slot_tiles, h2p, xs0)


def _expert_kernel(be_ref, bi_ref, nb_ref, xs_ref, wg_ref, bg_ref, wu_ref, bu_ref, wd_ref, bd_ref, yb_ref,
                   wg_scr, wu_scr, wd_scr):
    del bi_ref
    i = pl.program_id(0)

    @pl.when(jnp.logical_or(i == 0, be_ref[i] != be_ref[jnp.maximum(i - 1, 0)]))
    def _():
        wg_scr[...] = wg_ref[0].astype(BF16)
        wu_scr[...] = wu_ref[0].astype(BF16)
        wd_scr[...] = wd_ref[0].astype(BF16)

    @pl.when(i < nb_ref[0])
    def _():
        xb = xs_ref[...].astype(BF16)
        gt = _dot(xb, wg_scr[...]) + bg_ref[0]
        up = _dot(xb, wu_scr[...]) + bu_ref[0]
        gt = jnp.minimum(gt, SWIGLU_LIMIT)
        up = jnp.clip(up, -SWIGLU_LIMIT, SWIGLU_LIMIT)
        act = (up + 1.0) * (gt * _sigmoid(SWIGLU_ALPHA * gt))
        y = _dot(act.astype(BF16), wd_scr[...]) + bd_ref[0]
        yb_ref[...] = y

    @pl.when(pl.program_id(0) >= nb_ref[0])
    def _():
        yb_ref[...] = jnp.zeros_like(yb_ref)


def _experts(layer, block_e, block_i, n_used, xs, wg, bg, wu, bu, wd, bd):
    n_slots, width = xs.shape
    r = EXPERT_R
    depth, ne, d, dff = wg.shape
    wmap = lambda i, be, bi, nb: (layer, be[i], 0, 0)
    xmap = lambda i, be, bi, nb: (bi[i], 0)
    grid_spec = pltpu.PrefetchScalarGridSpec(
        num_scalar_prefetch=3,
        grid=(n_slots // r,),
        in_specs=[pl.BlockSpec((r, width), xmap),
                  pl.BlockSpec((None, 1, d, dff), wmap),
                  pl.BlockSpec((None, 1, 1, dff), wmap),
                  pl.BlockSpec((None, 1, d, dff), wmap),
                  pl.BlockSpec((None, 1, 1, dff), wmap),
                  pl.BlockSpec((None, 1, dff, d), wmap),
                  pl.BlockSpec((None, 1, 1, d), wmap)],
        out_specs=pl.BlockSpec((r, width), lambda i, be, bi, nb: (i, 0)),
        scratch_shapes=[pltpu.VMEM((d, dff), BF16), pltpu.VMEM((d, dff), BF16), pltpu.VMEM((dff, d), BF16)],
    )
    return pl.pallas_call(
        _expert_kernel,
        grid_spec=grid_spec,
        out_shape=jax.ShapeDtypeStruct((n_slots, width), F32),
        compiler_params=_params(("arbitrary",)),
        name="moe_experts",
    )(block_e, block_i, n_used, xs, wg, bg.reshape(depth, ne, 1, dff), wu, bu.reshape(depth, ne, 1, dff),
      wd, bd.reshape(depth, ne, 1, d))


def _combine_kernel(slot_hbm, yb_hbm, rp_ref, x1_ref, g2_ref, fg_ref, out_ref, buf, idx0, idx1, isem, sems, *, final):
    tm = x1_ref.shape[1]
    nt = pl.num_programs(1)
    i = pl.program_id(0) * nt + pl.program_id(1)
    last = pl.num_programs(0) * nt - 1
    cur = i % 2
    idx = (idx0, idx1)

    def idx_copy(tile, par):
        return pltpu.make_async_copy(slot_hbm.at[tile], idx[par], isem)

    def issue_gathers(par):
        def body(g, carry):
            base = pl.multiple_of(g * SUBLANES, SUBLANES)
            for rr in range(SUBLANES):
                for kk in range(TOP_K):
                    src = idx[par][(base + rr) * TOP_K + kk]
                    pltpu.make_async_copy(yb_hbm.at[pl.ds(src, 1)], buf.at[par, kk, pl.ds(base + rr, 1)],
                                          sems.at[par]).start(priority=kk % 2)
            return carry

        lax.fori_loop(0, tm // SUBLANES, body, 0)

    @pl.when(i == 0)
    def _():
        first = idx_copy(0, 0)
        first.start()
        first.wait()
        issue_gathers(0)
        idx_copy(jnp.minimum(1, last), 1).start()

    idx_copy(0, 0).wait()

    for par in range(2):
        @pl.when(jnp.logical_and(i < last, cur != par))
        def _():
            issue_gathers(par)

    for kk in range(TOP_K):
        _row_copy_wait(yb_hbm.at[pl.ds(0, tm)], buf.at[cur, kk], sems.at[cur])

    for par in range(2):
        @pl.when(jnp.logical_and(i < last, cur == par))
        def _():
            idx_copy(jnp.minimum(i + 2, last), par).start()

    rp = rp_ref[0]
    y = rp[:, 0:1] * buf[cur, 0]
    for kk in range(1, TOP_K):
        y = y + rp[:, kk:kk + 1] * buf[cur, kk]
    x2 = x1_ref[0] + g2_ref[0] * y
    if final:
        x2 = x2 * lax.rsqrt(jnp.mean(x2 * x2, axis=-1, keepdims=True) + NORM_EPS) * fg_ref[...]
    out_ref[0] = x2


def _combine(slot_tiles, yb, rp, x1, g2, fg, final):
    b, s, d = x1.shape
    tm = slot_tiles.shape[1] // TOP_K
    row = lambda i, j: (i, j, 0)
    return pl.pallas_call(
        functools.partial(_combine_kernel, final=final),
        grid=(b, s // tm),
        in_specs=[pl.BlockSpec(memory_space=pl.ANY),
                  pl.BlockSpec(memory_space=pl.ANY),
                  pl.BlockSpec((1, tm, LANES), row),
                  pl.BlockSpec((1, tm, d), row),
                  pl.BlockSpec((1, 1, d), lambda i, j: (i, 0, 0)),
                  pl.BlockSpec((1, d), lambda i, j: (0, 0))],
        out_specs=pl.BlockSpec((1, tm, d), row),
        out_shape=jax.ShapeDtypeStruct((b, s, d), F32),
        scratch_shapes=[pltpu.VMEM((2, TOP_K, tm, d), F32),
                        pltpu.SMEM((tm * TOP_K,), I32),
                        pltpu.SMEM((tm * TOP_K,), I32),
                        pltpu.SemaphoreType.DMA(()),
                        pltpu.SemaphoreType.DMA((2,))],
        compiler_params=_params(("arbitrary", "arbitrary")),
        name="moe_combine",
    )(slot_tiles, yb, rp, x1, g2, fg)


def _permute_w_in(w_in):
    d = w_in.shape[0]
    g0 = 4 * M_WIDTH
    a0 = g0 + 2 * M_HEADS
    pad = jnp.zeros((d, LANES - 2 * M_HEADS), w_in.dtype)
    return jnp.concatenate([w_in[:, :g0], w_in[:, a0:], w_in[:, g0:a0], pad], axis=1).astype(BF16)


def kernel(x, c, positions, w_ada, b_ada, norm1_g, w_in, conv_w, conv_b, b_igate, b_fgate, mlstm_norm_g,
           w_out, norm2_g, w_router, b_router, w_gate, b_gate, w_up, b_up, w_down, b_down, final_norm_g):
    b, s, d = x.shape
    depth = w_ada.shape[0]
    t = b * s
    tm = min(MOE_TM, s)
    n_slots = (t * TOP_K // EXPERT_R + N_EXPERTS) * EXPERT_R
    n_blocks = n_slots // EXPERT_R

    mod = _modulation(c, w_ada, b_ada)
    cos_t, sin_t = _rope_tables(positions)

    for l in range(depth):
        sh1, sc1, g1, sh2, sc2, g2 = [mod[l, :, i] for i in range(6)]
        gate_b = jnp.zeros((1, LANES), F32).at[0, :M_HEADS].set(b_igate[l]).at[0, M_HEADS:2 * M_HEADS].set(b_fgate[l])
        zqk, zv, zo, za, vt, zg, kmean = _in_projection(
            x, norm1_g[l].reshape(1, d), sc1, sh1, _permute_w_in(w_in[l]), conv_w[l],
            conv_b[l].reshape(1, -1), gate_b, cos_t, sin_t)
        hm = _mlstm(zqk, zv, zo, zg, mlstm_norm_g[l].reshape(1, -1))
        ha = _moba(za, vt, kmean.reshape(b, s // MOBA_BLOCK, A_WIDTH))

        wr = jnp.zeros((d, LANES), F32).at[:, :N_EXPERTS].set(w_router[l])
        wr_hi = wr.astype(BF16)
        wr_lo = (wr - wr_hi.astype(F32)).astype(BF16)
        b_r = jnp.full((1, LANES), NEG_BIG, F32).at[0, :N_EXPERTS].set(b_router[l])
        x1, h2p, ri, rp, cnt = _out_projection(
            hm, ha, w_out[l].astype(BF16), x, g1, norm2_g[l].reshape(1, d), sc2, sh2, wr_hi, wr_lo, b_r)

        counts = cnt[0, :N_EXPERTS].astype(I32)
        nblk = (counts + EXPERT_R - 1) // EXPERT_R
        blk_end = jnp.cumsum(nblk)
        blk_start = blk_end - nblk
        n_used = blk_end[-1:]
        block_i = jnp.minimum(jnp.arange(n_blocks, dtype=I32), n_used[0] - 1)
        block_e = jnp.minimum(jnp.sum((blk_end[None, :] <= block_i[:, None]).astype(I32), axis=1), N_EXPERTS - 1)
        eid = ri[..., :TOP_K].reshape(t, TOP_K)
        rank = ri[..., TOP_K:2 * TOP_K].reshape(t, TOP_K)
        slot = blk_start[eid] * EXPERT_R + rank
        slot_tiles = slot.reshape(t // tm, tm * TOP_K)

        xs = _scatter_rows(slot_tiles, h2p.reshape(t, d), n_slots)
        yb = _experts(l, block_e, block_i, n_used.astype(I32), xs, w_gate, b_gate, w_up, b_up, w_down, b_down)
        x = _combine(slot_tiles, yb, rp, x1, g2, final_norm_g.reshape(1, d), final=(l == depth - 1))
    return x
```

```python
import functools

import jax
import jax.numpy as jnp
from jax import lax
from jax.experimental import pallas as pl
from jax.experimental.pallas import tpu as pltpu

F32 = jnp.float32
BF16 = jnp.bfloat16
I32 = jnp.int32

M_HEADS = 4
M_HEAD_DIM = 128
M_WIDTH = M_HEADS * M_HEAD_DIM
A_HEADS = 8
A_HEAD_DIM = 64
A_WIDTH = A_HEADS * A_HEAD_DIM
CONV_K = 4
GATE_SOFTCAP = 15.0
MOBA_BLOCK = 256
MOBA_TOPK = 3
ROPE_THETA = 500000.0
ROPE_DIM = A_HEAD_DIM // 4
N_EXPERTS = 32
TOP_K = 4
SWIGLU_LIMIT = 7.0
SWIGLU_ALPHA = 1.702
NORM_EPS = 1e-5

LANES = 128
SUBLANES = 8
VMEM_LIMIT = 56 * 1024 * 1024

NEG_BIG = -1e30
MLSTM_L = 256
IN_TM = 256
OUT_TM = 512
MOE_TM = 512
EXPERT_R = 512
VT_ROWS = A_HEAD_DIM + 16
LOG2E = 1.4426950408889634

C_QK = 0
C_V = 2 * M_WIDTH
C_O = 3 * M_WIDTH
C_A = 4 * M_WIDTH
C_G = 4 * M_WIDTH + 3 * A_WIDTH
N_COLS = C_G + LANES


def _dot(a, b):
    return jnp.dot(a, b, preferred_element_type=F32)


def _dot_nt(a, b):
    return lax.dot_general(a, b, (((1,), (1,)), ((), ())), preferred_element_type=F32)


def _split(a):
    hi = a.astype(BF16)
    lo = (a - hi.astype(F32)).astype(BF16)
    return hi, lo


def _dot3(a, b):
    ah, al = _split(a)
    bh, bl = _split(b)
    return _dot(ah, bh) + (_dot(al, bh) + _dot(ah, bl))


def _sigmoid(t):
    return 1.0 / (1.0 + jnp.exp(-t))


def _params(sem):
    return pltpu.CompilerParams(dimension_semantics=sem, vmem_limit_bytes=VMEM_LIMIT)


def _mod_kernel(c_ref, w_ref, b_ref, o_ref):
    c = c_ref[...]
    cond = c * _sigmoid(c)
    o_ref[0] = _dot3(cond, w_ref[0]) + b_ref[0]


def _modulation(c, w_ada, b_ada):
    depth, d, n6 = w_ada.shape
    b = c.shape[0]
    rows = -(-b // SUBLANES) * SUBLANES
    cp = jnp.zeros((rows, d), F32).at[:b].set(c)
    tn = 1536
    out = pl.pallas_call(
        _mod_kernel,
        grid=(depth, n6 // tn),
        in_specs=[pl.BlockSpec((rows, d), lambda l, j: (0, 0)),
                  pl.BlockSpec((1, d, tn), lambda l, j: (l, 0, j)),
                  pl.BlockSpec((1, 1, tn), lambda l, j: (l, 0, j))],
        out_specs=pl.BlockSpec((1, rows, tn), lambda l, j: (l, 0, j)),
        out_shape=jax.ShapeDtypeStruct((depth, rows, n6), F32),
        compiler_params=_params(("arbitrary", "arbitrary")),
        name="adaln_mod",
    )(cp, w_ada, b_ada.reshape(depth, 1, n6))
    return out[:, :b].reshape(depth, b, 6, 1, d)


def _rope_kernel(pos_ref, invf_ref, sign_ref, cos_ref, sin_ref):
    ang = pos_ref[0].astype(F32) * invf_ref[...]
    cos_ref[0] = jnp.cos(ang)
    sin_ref[0] = jnp.sin(ang) * sign_ref[...]


def _rope_tables(positions):
    b, s = positions.shape
    half = ROPE_DIM // 2
    inv_freq = ROPE_THETA ** (-jnp.arange(0, ROPE_DIM, 2, dtype=F32) / ROPE_DIM)
    lane = jnp.arange(LANES) % A_HEAD_DIM
    invf = jnp.where(lane < ROPE_DIM, inv_freq[lane % half], 0.0).astype(F32).reshape(1, LANES)
    sign = jnp.where(lane < half, -1.0, jnp.where(lane < ROPE_DIM, 1.0, 0.0)).astype(F32).reshape(1, LANES)
    ts = min(s, 1024)
    tab = jax.ShapeDtypeStruct((b, s, LANES), F32)
    return pl.pallas_call(
        _rope_kernel,
        grid=(b, s // ts),
        in_specs=[pl.BlockSpec((1, ts, 1), lambda i, j: (i, j, 0)),
                  pl.BlockSpec((1, LANES), lambda i, j: (0, 0)),
                  pl.BlockSpec((1, LANES), lambda i, j: (0, 0))],
        out_specs=[pl.BlockSpec((1, ts, LANES), lambda i, j: (i, j, 0))] * 2,
        out_shape=[tab, tab],
        compiler_params=_params(("arbitrary", "arbitrary")),
        name="rope_tables",
    )(positions.reshape(b, s, 1), invf, sign)


def _inproj_kernel(x_ref, g_ref, sc_ref, sh_ref, w_ref, cw_ref, cb_ref, gb_ref, cos_ref, sin_ref,
                   zqk_ref, zv_ref, zo_ref, za_ref, vt_ref, zg_ref, km_ref, halo_ref):
    tm = x_ref.shape[1]
    x = x_ref[0]
    var = jnp.mean(x * x, axis=-1, keepdims=True)
    h = x * lax.rsqrt(var + NORM_EPS) * g_ref[...]
    h = h * (1.0 + sc_ref[0]) + sh_ref[0]
    hb = h.astype(BF16)

    @pl.when(pl.program_id(1) == 0)
    def _():
        halo_ref[...] = jnp.zeros_like(halo_ref)

    z = _dot(hb, w_ref[:, C_QK:C_V])
    prev = halo_ref[...]
    row8 = lax.broadcasted_iota(I32, prev.shape, 0)
    acc = z * cw_ref[CONV_K - 1:CONV_K, :] + cb_ref[...]
    for k in range(1, CONV_K):
        zs = pltpu.roll(z, k, 0)
        ps = pltpu.roll(prev, k, 0)
        head = jnp.where(row8 < k, ps, zs[:SUBLANES])
        zs = jnp.concatenate([head, zs[SUBLANES:]], axis=0)
        acc = acc + zs * cw_ref[CONV_K - 1 - k:CONV_K - k, :]
    halo_ref[...] = z[tm - SUBLANES:, :]
    qk = acc * _sigmoid(acc)
    col = lax.broadcasted_iota(I32, (1, 2 * M_WIDTH), 1)
    kscale = jnp.where(col < M_WIDTH, 1.0, M_HEAD_DIM ** -0.5).astype(F32)
    zqk_ref[0] = (qk * kscale).astype(BF16)

    zv_ref[0] = _dot(hb, w_ref[:, C_V:C_O]).astype(BF16)
    zo_ref[0] = _dot(hb, w_ref[:, C_O:C_A]).astype(BF16)

    za = _dot(hb, w_ref[:, C_A:C_G])
    cos = cos_ref[0]
    sin = sin_ref[0]
    lane = lax.broadcasted_iota(I32, (tm, LANES), 1)
    first_half = (lane % A_HEAD_DIM) < (ROPE_DIM // 2)
    for c in range(2 * A_WIDTH // LANES):
        xc = za[:, c * LANES:(c + 1) * LANES]
        partner = jnp.where(first_half, pltpu.roll(xc, LANES - ROPE_DIM // 2, 1), pltpu.roll(xc, ROPE_DIM // 2, 1))
        rc = xc * cos + partner * sin
        za_ref[0, :, c * LANES:(c + 1) * LANES] = rc.astype(BF16)
        if c >= A_WIDTH // LANES:
            kc = c - A_WIDTH // LANES
            km_ref[0, 0, :, kc * LANES:(kc + 1) * LANES] = jnp.mean(rc, axis=0, keepdims=True)
    for c in range(A_WIDTH // LANES):
        vc = za[:, 2 * A_WIDTH + c * LANES:2 * A_WIDTH + (c + 1) * LANES]
        vct = vc.T
        extra = (lax.broadcasted_iota(I32, (VT_ROWS - A_HEAD_DIM, tm), 0) == 0).astype(F32)
        for hd in range(LANES // A_HEAD_DIM):
            head_t = vct[hd * A_HEAD_DIM:(hd + 1) * A_HEAD_DIM, :]
            vt_ref[0, c * (LANES // A_HEAD_DIM) + hd, 0] = jnp.concatenate([head_t, extra], axis=0).astype(BF16)

    t = _dot(hb, w_ref[:, C_G:N_COLS]) + gb_ref[...]
    t = GATE_SOFTCAP * jnp.tanh(t / GATE_SOFTCAP)
    logsig = jnp.minimum(t, 0.0) - jnp.log(1.0 + jnp.exp(-jnp.abs(t)))
    zg_ref[0] = jnp.where(lane < M_HEADS, t, logsig)


def _in_projection(x, g, sc, sh, w_perm, conv_w, conv_b, gate_b, cos_t, sin_t):
    b, s, d = x.shape
    tm = IN_TM
    assert tm == MOBA_BLOCK and s % tm == 0
    nt = s // tm
    row = lambda i, j: (i, j, 0)
    bcast = lambda i, j: (i, 0, 0)
    const = lambda i, j: (0, 0)
    out_shape = [
        jax.ShapeDtypeStruct((b, s, 2 * M_WIDTH), BF16),
        jax.ShapeDtypeStruct((b, s, M_WIDTH), BF16),
        jax.ShapeDtypeStruct((b, s, M_WIDTH), BF16),
        jax.ShapeDtypeStruct((b, s, 2 * A_WIDTH), BF16),
        jax.ShapeDtypeStruct((b, A_HEADS, nt, VT_ROWS, tm), BF16),
        jax.ShapeDtypeStruct((b, s, LANES), F32),
        jax.ShapeDtypeStruct((b, nt, 1, A_WIDTH), F32),
    ]
    out_specs = [
        pl.BlockSpec((1, tm, 2 * M_WIDTH), row),
        pl.BlockSpec((1, tm, M_WIDTH), row),
        pl.BlockSpec((1, tm, M_WIDTH), row),
        pl.BlockSpec((1, tm, 2 * A_WIDTH), row),
        pl.BlockSpec((1, A_HEADS, 1, VT_ROWS, tm), lambda i, j: (i, 0, j, 0, 0)),
        pl.BlockSpec((1, tm, LANES), row),
        pl.BlockSpec((1, 1, 1, A_WIDTH), lambda i, j: (i, j, 0, 0)),
    ]
    in_specs = [
        pl.BlockSpec((1, tm, d), row),
        pl.BlockSpec((1, d), const),
        pl.BlockSpec((1, 1, d), bcast),
        pl.BlockSpec((1, 1, d), bcast),
        pl.BlockSpec((d, N_COLS), const),
        pl.BlockSpec((CONV_K, 2 * M_WIDTH), const),
        pl.BlockSpec((1, 2 * M_WIDTH), const),
        pl.BlockSpec((1, LANES), const),
        pl.BlockSpec((1, tm, LANES), row),
        pl.BlockSpec((1, tm, LANES), row),
    ]
    return pl.pallas_call(
        _inproj_kernel,
        grid=(b, nt),
        in_specs=in_specs,
        out_specs=out_specs,
        out_shape=out_shape,
        scratch_shapes=[pltpu.VMEM((SUBLANES, 2 * M_WIDTH), F32)],
        compiler_params=_params(("arbitrary", "arbitrary")),
        name="in_proj",
    )(x, g, sc, sh, w_perm, conv_w, conv_b, gate_b, cos_t, sin_t)


def _mlstm_kernel(qk_ref, v_ref, o_ref, g_ref, ng_ref, out_ref, c_scr, n_scr, m_scr):
    L = qk_ref.shape[1]

    @pl.when(pl.program_id(1) == 0)
    def _():
        c_scr[...] = jnp.zeros_like(c_scr)
        n_scr[...] = jnp.zeros_like(n_scr)
        m_scr[...] = jnp.full_like(m_scr, -jnp.inf)

    gates = g_ref[0]
    r = lax.broadcasted_iota(I32, (L, L), 0)
    c = lax.broadcasted_iota(I32, (L, L), 1)
    causal = c <= r
    tri = causal.astype(BF16)
    gh, gl = _split(gates)
    cum = _dot(tri, gh) + _dot(tri, gl)
    lane = lax.broadcasted_iota(I32, gates.shape, 1)
    mix = jnp.where(lane < M_HEADS, gates, cum)
    mix_t = mix.T

    for hd in range(M_HEADS):
        q = qk_ref[0, :, hd * M_HEAD_DIM:(hd + 1) * M_HEAD_DIM]
        k = qk_ref[0, :, M_WIDTH + hd * M_HEAD_DIM:M_WIDTH + (hd + 1) * M_HEAD_DIM]
        v = v_ref[0, :, hd * M_HEAD_DIM:(hd + 1) * M_HEAD_DIM]
        ig_col = mix[:, hd:hd + 1]
        b_col = mix[:, M_HEADS + hd:M_HEADS + hd + 1]
        ig_row = mix_t[hd:hd + 1, :]
        b_row = mix_t[M_HEADS + hd:M_HEADS + hd + 1, :]
        g_tot = b_row[:, L - 1:L]
        m_prev = m_scr[hd, 0:1, 0:1]
        c_prev = c_scr[hd]
        n_prev = n_scr[hd, 0:1, :]

        a_col = b_col + m_prev
        dmat = jnp.where(causal, b_col - b_row + ig_row, -jnp.inf)
        m_t = jnp.maximum(a_col, jnp.max(dmat, axis=1, keepdims=True))
        s = _dot_nt(q, k) * jnp.exp(dmat - m_t)
        inter = jnp.exp(a_col - m_t)
        num = _dot(s.astype(BF16), v) + inter * _dot(q, c_prev.astype(BF16))
        qn = jnp.sum(q.astype(F32) * n_prev, axis=1, keepdims=True)
        den = jnp.sum(s, axis=1, keepdims=True) + inter * qn
        hh = num / jnp.maximum(jnp.abs(den), jnp.exp(-m_t))

        w_row = g_tot - b_row + ig_row
        m_new = jnp.maximum(g_tot + m_prev, jnp.max(w_row, axis=1, keepdims=True))
        decay = jnp.exp(g_tot + m_prev - m_new)
        kw = k.astype(F32) * jnp.exp(g_tot - b_col + ig_col - m_new)
        c_scr[hd] = decay * c_prev + _dot(kw.T.astype(BF16), v)
        n_scr[hd] = jnp.broadcast_to(decay * n_prev + jnp.sum(kw, axis=0, keepdims=True), n_scr.shape[1:])
        m_scr[hd] = jnp.broadcast_to(m_new, m_scr.shape[1:])

        og = _sigmoid(o_ref[0, :, hd * M_HEAD_DIM:(hd + 1) * M_HEAD_DIM].astype(F32))
        ho = hh * og
        hn = ho * lax.rsqrt(jnp.mean(ho * ho, axis=-1, keepdims=True) + NORM_EPS)
        out_ref[0, :, hd * M_HEAD_DIM:(hd + 1) * M_HEAD_DIM] = (
            hn * ng_ref[:, hd * M_HEAD_DIM:(hd + 1) * M_HEAD_DIM]).astype(BF16)


def _mlstm(zqk, zv, zo, zg, norm_g):
    b, s, _ = zqk.shape
    L = min(MLSTM_L, s)
    row = lambda i, j: (i, j, 0)
    return pl.pallas_call(
        _mlstm_kernel,
        grid=(b, s // L),
        in_specs=[pl.BlockSpec((1, L, 2 * M_WIDTH), row),
                  pl.BlockSpec((1, L, M_WIDTH), row),
                  pl.BlockSpec((1, L, M_WIDTH), row),
                  pl.BlockSpec((1, L, LANES), row),
                  pl.BlockSpec((1, M_WIDTH), lambda i, j: (0, 0))],
        out_specs=pl.BlockSpec((1, L, M_WIDTH), row),
        out_shape=jax.ShapeDtypeStruct((b, s, M_WIDTH), BF16),
        scratch_shapes=[pltpu.VMEM((M_HEADS, M_HEAD_DIM, M_HEAD_DIM), F32),
                        pltpu.VMEM((M_HEADS, SUBLANES, M_HEAD_DIM), F32),
                        pltpu.VMEM((M_HEADS, SUBLANES, LANES), F32)],
        compiler_params=_params(("arbitrary", "arbitrary")),
        name="mlstm",
    )(zqk, zv, zo, zg, norm_g)


def _moba_kernel(q_ref, k_ref, vt_ref, km_ref, o_ref, qs_scr, kill_scr, acc_scr, st_scr, st2_scr):
    tq = q_ref.shape[1]
    nb = km_ref.shape[1]
    own = pl.program_id(1)
    pairs = q_ref.shape[2] // LANES
    per = LANES // A_HEAD_DIM
    chains = [(hp, hd) for hp in range(pairs) for hd in range(per)]

    lane = lax.broadcasted_iota(I32, (tq, LANES), 1)
    head0 = lane < A_HEAD_DIM
    blk = lax.broadcasted_iota(I32, (nb, per * tq), 0)
    for hp in range(pairs):
        q = q_ref[0, :, hp * LANES:(hp + 1) * LANES]
        zero = jnp.zeros_like(q)
        q2 = jnp.concatenate([jnp.where(head0, q, zero), jnp.where(head0, zero, q)], axis=0)
        qs_scr[hp] = (q2.astype(F32) * (A_HEAD_DIM ** -0.5 * LOG2E)).astype(BF16)

        kmh, kml = _split(km_ref[0, :, hp * LANES:(hp + 1) * LANES])
        gate = _dot_nt(kmh, q2) + _dot_nt(kml, q2)
        g0 = jnp.where(blk < own, gate, -jnp.inf)
        sel = jnp.zeros(gate.shape, jnp.bool_)
        for _ in range(MOBA_TOPK):
            mx = jnp.max(g0, axis=0, keepdims=True)
            hit = jnp.logical_and(g0 == mx, mx > -jnp.inf)
            idx = jnp.min(jnp.where(hit, blk, nb), axis=0, keepdims=True)
            pick = blk == idx
            sel = jnp.logical_or(sel, pick)
            g0 = jnp.where(pick, -jnp.inf, g0)
        kill_scr[:, hp * per * tq:(hp + 1) * per * tq] = jnp.where(sel, 0.0, -NEG_BIG).astype(F32)

    def scores(kj, hp, hd):
        return _dot_nt(kj[:, hp * LANES:(hp + 1) * LANES], qs_scr[hp, hd * tq:(hd + 1) * tq, :])

    def values(j, c, p):
        return _dot(vt_ref[0, c, j], p.astype(BF16))

    off = pl.multiple_of(own * tq, tq)
    kj = k_ref[0, pl.ds(off, tq), :]
    key = lax.broadcasted_iota(I32, (tq, tq), 0)
    qry = lax.broadcasted_iota(I32, (tq, tq), 1)
    sts = [jnp.where(key <= qry, scores(kj, hp, hd), NEG_BIG) for hp, hd in chains]
    state = []
    for c in range(len(chains)):
        m = jnp.max(sts[c], axis=0, keepdims=True)
        state.append(m)
        acc_scr[c] = values(own, c, jnp.exp2(sts[c] - m))

    def stage_scores(j, dst):
        offj = pl.multiple_of(j * tq, tq)
        kj = k_ref[0, pl.ds(offj, tq), :]
        for c, (hp, hd) in enumerate(chains):
            dst[c] = scores(kj, hp, hd)

    def step(j, src, dst, carry):
        stage_scores(j + 1, dst)
        kill = kill_scr[pl.ds(j, 1), :]
        acc_prev = [acc_scr[c] for c in range(len(chains))]
        out, alphas, pvs = [], [], []
        for c in range(len(chains)):
            m = carry[c]
            kh = kill[:, c * tq:(c + 1) * tq]
            mn = jnp.maximum(m, jnp.max(src[c], axis=0, keepdims=True) - kh)
            out.append(mn)
            alphas.append(jnp.exp2(m - mn))
            pvs.append(values(j, c, jnp.exp2(src[c] - (mn + kh))))
        for c in range(len(chains)):
            acc_scr[c] = alphas[c] * acc_prev[c] + pvs[c]
        return tuple(out)

    stage_scores(0, st_scr)

    def body(i, carry):
        carry = step(2 * i, st_scr, st2_scr, carry)
        return step(2 * i + 1, st2_scr, st_scr, carry)

    state = lax.fori_loop(0, own // 2, body, tuple(state))
    lax.cond(own % 2 == 1, lambda st: step(own - 1, st_scr, st2_scr, st), lambda st: st, state)
    for hp in range(pairs):
        heads_t = []
        for hd in range(per):
            acc = acc_scr[hp * per + hd]
            heads_t.append(acc[:A_HEAD_DIM] / acc[A_HEAD_DIM:A_HEAD_DIM + 1])
        o_ref[0, :, hp * LANES:(hp + 1) * LANES] = jnp.concatenate(heads_t, axis=0).T.astype(BF16)


def _moba(za, vt, kmean):
    b, s, _ = za.shape
    tq = MOBA_BLOCK
    nb = s // tq
    pairs = A_WIDTH // LANES
    once = pl.Buffered(1)
    return pl.pallas_call(
        _moba_kernel,
        grid=(b, nb),
        in_specs=[pl.BlockSpec((1, tq, A_WIDTH), lambda i, j: (i, j, 0)),
                  pl.BlockSpec((1, s, A_WIDTH), lambda i, j: (i, 0, 1), pipeline_mode=once),
                  pl.BlockSpec((1, A_HEADS, nb, VT_ROWS, tq), lambda i, j: (i, 0, 0, 0, 0), pipeline_mode=once),
                  pl.BlockSpec((1, nb, A_WIDTH), lambda i, j: (i, 0, 0))],
        out_specs=pl.BlockSpec((1, tq, A_WIDTH), lambda i, j: (i, j, 0)),
        out_shape=jax.ShapeDtypeStruct((b, s, A_WIDTH), BF16),
        scratch_shapes=[pltpu.VMEM((pairs, 2 * tq, LANES), BF16),
                        pltpu.VMEM((nb, 2 * pairs * tq), F32),
                        pltpu.VMEM((A_HEADS, VT_ROWS, tq), F32),
                        pltpu.VMEM((A_HEADS, tq, tq), F32),
                        pltpu.VMEM((A_HEADS, tq, tq), F32)],
        compiler_params=_params(("arbitrary", "arbitrary")),
        name="moba",
    )(za, za, vt, kmean)


def _outproj_kernel(hm_ref, ha_ref, wo_ref, x_ref, g1_ref, n2_ref, sc_ref, sh_ref, wrh_ref, wrl_ref, br_ref,
                    x1_ref, h2p_ref, ri_ref, rp_ref, cnt_ref, cnt_scr):
    tm = x_ref.shape[1]

    @pl.when(jnp.logical_and(pl.program_id(0) == 0, pl.program_id(1) == 0))
    def _():
        cnt_scr[...] = jnp.zeros_like(cnt_scr)

    y = _dot(hm_ref[0], wo_ref[:M_WIDTH, :]) + _dot(ha_ref[0], wo_ref[M_WIDTH:, :])
    x1 = x_ref[0] + g1_ref[0] * y
    x1_ref[0] = x1
    var = jnp.mean(x1 * x1, axis=-1, keepdims=True)
    h2 = x1 * lax.rsqrt(var + NORM_EPS) * n2_ref[...]
    h2 = h2 * (1.0 + sc_ref[0]) + sh_ref[0]
    h2p_ref[0] = h2

    hh, hl = _split(h2)
    logits = _dot(hh, wrh_ref[...]) + (_dot(hl, wrh_ref[...]) + _dot(hh, wrl_ref[...])) + br_ref[...]
    lane = lax.broadcasted_iota(I32, logits.shape, 1)
    g0 = logits
    vals, idxs = [], []
    for _ in range(TOP_K):
        mx = jnp.max(g0, axis=1, keepdims=True)
        idx = jnp.min(jnp.where(g0 == mx, lane, LANES), axis=1, keepdims=True)
        vals.append(mx)
        idxs.append(idx)
        g0 = jnp.where(lane == idx, -jnp.inf, g0)
    ex = [jnp.exp(vv - vals[0]) for vv in vals]
    tot = ex[0] + ex[1] + ex[2] + ex[3]

    onehot = jnp.zeros(logits.shape, F32)
    for idx in idxs:
        onehot = onehot + (lane == idx).astype(F32)
    r = lax.broadcasted_iota(I32, (tm, tm), 0)
    c = lax.broadcasted_iota(I32, (tm, tm), 1)
    before = _dot((c < r).astype(BF16), onehot.astype(BF16)) + cnt_scr[0:1, :]
    ri = jnp.zeros(logits.shape, I32)
    rp = jnp.zeros(logits.shape, F32)
    for kk in range(TOP_K):
        rank = jnp.sum(jnp.where(lane == idxs[kk], before, 0.0), axis=1, keepdims=True)
        ri = jnp.where(lane == kk, idxs[kk], ri)
        ri = jnp.where(lane == TOP_K + kk, rank.astype(I32), ri)
        rp = jnp.where(lane == kk, ex[kk] / tot, rp)
    ri_ref[0] = ri
    rp_ref[0] = rp
    cnt = cnt_scr[...] + jnp.sum(onehot, axis=0, keepdims=True)
    cnt_scr[...] = cnt
    cnt_ref[...] = cnt


def _out_projection(hm, ha, w_out, x, g1, n2g, sc2, sh2, wr_hi, wr_lo, b_r):
    b, s, d = x.shape
    tm = min(OUT_TM, s)
    row = lambda i, j: (i, j, 0)
    bcast = lambda i, j: (i, 0, 0)
    const = lambda i, j: (0, 0)
    out_shape = [
        jax.ShapeDtypeStruct((b, s, d), F32),
        jax.ShapeDtypeStruct((b, s, d), F32),
        jax.ShapeDtypeStruct((b, s, LANES), I32),
        jax.ShapeDtypeStruct((b, s, LANES), F32),
        jax.ShapeDtypeStruct((SUBLANES, LANES), F32),
    ]
    out_specs = [
        pl.BlockSpec((1, tm, d), row),
        pl.BlockSpec((1, tm, d), row),
        pl.BlockSpec((1, tm, LANES), row),
        pl.BlockSpec((1, tm, LANES), row),
        pl.BlockSpec((SUBLANES, LANES), const),
    ]
    in_specs = [
        pl.BlockSpec((1, tm, M_WIDTH), row),
        pl.BlockSpec((1, tm, A_WIDTH), row),
        pl.BlockSpec((M_WIDTH + A_WIDTH, d), const),
        pl.BlockSpec((1, tm, d), row),
        pl.BlockSpec((1, 1, d), bcast),
        pl.BlockSpec((1, d), const),
        pl.BlockSpec((1, 1, d), bcast),
        pl.BlockSpec((1, 1, d), bcast),
        pl.BlockSpec((d, LANES), const),
        pl.BlockSpec((d, LANES), const),
        pl.BlockSpec((1, LANES), const),
    ]
    return pl.pallas_call(
        _outproj_kernel,
        grid=(b, s // tm),
        in_specs=in_specs,
        out_specs=out_specs,
        out_shape=out_shape,
        scratch_shapes=[pltpu.VMEM((SUBLANES, LANES), F32)],
        compiler_params=_params(("arbitrary", "arbitrary")),
        name="out_proj_router",
    )(hm, ha, w_out, x, g1, n2g, sc2, sh2, wr_hi, wr_lo, b_r)


def _row_copy_wait(src, dst, sem):
    pltpu.make_async_copy(src, dst, sem).wait()


def _scatter_kernel(zl_ref, nz_ref, slot_hbm, h2p_ref, xs_out, idx_smem, zbuf, isem, sem, zsem):
    tm = h2p_ref.shape[0]
    zr = zbuf.shape[0]
    i = pl.program_id(0)

    @pl.when(i == 0)
    def _():
        zbuf[...] = jnp.zeros_like(zbuf)

        def fill(z, carry):
            row0 = pl.multiple_of(zl_ref[z] * zr, zr)
            pltpu.make_async_copy(zbuf, xs_out.at[pl.ds(row0, zr)], zsem).start()
            return carry

        def drain(z, carry):
            pltpu.make_async_copy(zbuf, xs_out.at[pl.ds(0, zr)], zsem).wait()
            return carry

        lax.fori_loop(0, nz_ref[0], fill, 0)
        lax.fori_loop(0, nz_ref[0], drain, 0)

    cp = pltpu.make_async_copy(slot_hbm.at[i], idx_smem, isem)
    cp.start()
    cp.wait()

    def body(g, carry):
        base = pl.multiple_of(g * SUBLANES, SUBLANES)
        for rr in range(SUBLANES):
            for kk in range(TOP_K):
                dst = idx_smem[(base + rr) * TOP_K + kk]
                pltpu.make_async_copy(h2p_ref.at[pl.ds(base + rr, 1)], xs_out.at[pl.ds(dst, 1)],
                                      sem).start(priority=kk % 2)
        return carry

    lax.fori_loop(0, tm // SUBLANES, body, 0)
    for kk in range(TOP_K):
        _row_copy_wait(h2p_ref, xs_out.at[pl.ds(0, tm)], sem)


def _scatter_rows(zero_blocks, n_zero, slot_tiles, h2p, n_slots):
    t, w = h2p.shape
    tm = slot_tiles.shape[1] // TOP_K
    grid_spec = pltpu.PrefetchScalarGridSpec(
        num_scalar_prefetch=2,
        grid=(t // tm,),
        in_specs=[pl.BlockSpec(memory_space=pl.ANY),
                  pl.BlockSpec((tm, w), lambda i, zl, nz: (i, 0))],
        out_specs=pl.BlockSpec(memory_space=pl.ANY),
        scratch_shapes=[pltpu.SMEM((tm * TOP_K,), I32),
                        pltpu.VMEM((EXPERT_R, w), h2p.dtype),
                        pltpu.SemaphoreType.DMA(()),
                        pltpu.SemaphoreType.DMA(()),
                        pltpu.SemaphoreType.DMA(())],
    )
    return pl.pallas_call(
        _scatter_kernel,
        grid_spec=grid_spec,
        out_shape=jax.ShapeDtypeStruct((n_slots, w), h2p.dtype),
        compiler_params=_params(("arbitrary",)),
        name="moe_scatter",
    )(zero_blocks, n_zero, slot_tiles, h2p)


def _expert_kernel(be_ref, bi_ref, nb_ref, xs_ref, wg_ref, bg_ref, wu_ref, bu_ref, wd_ref, bd_ref, yb_ref,
                   wg_scr, wu_scr, wd_scr):
    del bi_ref
    i = pl.program_id(0)

    @pl.when(jnp.logical_or(i == 0, be_ref[i] != be_ref[jnp.maximum(i - 1, 0)]))
    def _():
        wg_scr[...] = wg_ref[0].astype(BF16)
        wu_scr[...] = wu_ref[0].astype(BF16)
        wd_scr[...] = wd_ref[0].astype(BF16)

    @pl.when(i < nb_ref[0])
    def _():
        xb = xs_ref[...].astype(BF16)
        gt = _dot(xb, wg_scr[...]) + bg_ref[0]
        up = _dot(xb, wu_scr[...]) + bu_ref[0]
        gt = jnp.minimum(gt, SWIGLU_LIMIT)
        up = jnp.clip(up, -SWIGLU_LIMIT, SWIGLU_LIMIT)
        act = (up + 1.0) * (gt * _sigmoid(SWIGLU_ALPHA * gt))
        y = _dot(act.astype(BF16), wd_scr[...]) + bd_ref[0]
        yb_ref[...] = y

    @pl.when(pl.program_id(0) >= nb_ref[0])
    def _():
        yb_ref[...] = jnp.zeros_like(yb_ref)


def _experts(layer, block_e, block_i, n_used, xs, wg, bg, wu, bu, wd, bd):
    n_slots, width = xs.shape
    r = EXPERT_R
    depth, ne, d, dff = wg.shape
    wmap = lambda i, be, bi, nb: (layer, be[i], 0, 0)
    xmap = lambda i, be, bi, nb: (bi[i], 0)
    grid_spec = pltpu.PrefetchScalarGridSpec(
        num_scalar_prefetch=3,
        grid=(n_slots // r,),
        in_specs=[pl.BlockSpec((r, width), xmap),
                  pl.BlockSpec((None, 1, d, dff), wmap),
                  pl.BlockSpec((None, 1, 1, dff), wmap),
                  pl.BlockSpec((None, 1, d, dff), wmap),
                  pl.BlockSpec((None, 1, 1, dff), wmap),
                  pl.BlockSpec((None, 1, dff, d), wmap),
                  pl.BlockSpec((None, 1, 1, d), wmap)],
        out_specs=pl.BlockSpec((r, width), lambda i, be, bi, nb: (i, 0)),
        scratch_shapes=[pltpu.VMEM((d, dff), BF16), pltpu.VMEM((d, dff), BF16), pltpu.VMEM((dff, d), BF16)],
    )
    return pl.pallas_call(
        _expert_kernel,
        grid_spec=grid_spec,
        out_shape=jax.ShapeDtypeStruct((n_slots, width), F32),
        compiler_params=_params(("arbitrary",)),
        name="moe_experts",
    )(block_e, block_i, n_used, xs, wg, bg.reshape(depth, ne, 1, dff), wu, bu.reshape(depth, ne, 1, dff),
      wd, bd.reshape(depth, ne, 1, d))


def _combine_kernel(slot_hbm, yb_hbm, rp_ref, x1_ref, g2_ref, fg_ref, out_ref, buf, idx0, idx1, isem, sems, *, final):
    tm = x1_ref.shape[1]
    nt = pl.num_programs(1)
    i = pl.program_id(0) * nt + pl.program_id(1)
    last = pl.num_programs(0) * nt - 1
    cur = i % 2
    idx = (idx0, idx1)

    def idx_copy(tile, par):
        return pltpu.make_async_copy(slot_hbm.at[tile], idx[par], isem)

    def issue_gathers(par):
        def body(g, carry):
            base = pl.multiple_of(g * SUBLANES, SUBLANES)
            for rr in range(SUBLANES):
                for kk in range(TOP_K):
                    src = idx[par][(base + rr) * TOP_K + kk]
                    pltpu.make_async_copy(yb_hbm.at[pl.ds(src, 1)], buf.at[par, kk, pl.ds(base + rr, 1)],
                                          sems.at[par]).start(priority=kk % 2)
            return carry

        lax.fori_loop(0, tm // SUBLANES, body, 0)

    @pl.when(i == 0)
    def _():
        first = idx_copy(0, 0)
        first.start()
        first.wait()
        issue_gathers(0)
        idx_copy(jnp.minimum(1, last), 1).start()

    idx_copy(0, 0).wait()

    for par in range(2):
        @pl.when(jnp.logical_and(i < last, cur != par))
        def _():
            issue_gathers(par)

    for kk in range(TOP_K):
        _row_copy_wait(yb_hbm.at[pl.ds(0, tm)], buf.at[cur, kk], sems.at[cur])

    for par in range(2):
        @pl.when(jnp.logical_and(i < last, cur == par))
        def _():
            idx_copy(jnp.minimum(i + 2, last), par).start()

    rp = rp_ref[0]
    y = rp[:, 0:1] * buf[cur, 0]
    for kk in range(1, TOP_K):
        y = y + rp[:, kk:kk + 1] * buf[cur, kk]
    x2 = x1_ref[0] + g2_ref[0] * y
    if final:
        x2 = x2 * lax.rsqrt(jnp.mean(x2 * x2, axis=-1, keepdims=True) + NORM_EPS) * fg_ref[...]
    out_ref[0] = x2


def _combine(slot_tiles, yb, rp, x1, g2, fg, final):
    b, s, d = x1.shape
    tm = slot_tiles.shape[1] // TOP_K
    row = lambda i, j: (i, j, 0)
    return pl.pallas_call(
        functools.partial(_combine_kernel, final=final),
        grid=(b, s // tm),
        in_specs=[pl.BlockSpec(memory_space=pl.ANY),
                  pl.BlockSpec(memory_space=pl.ANY),
                  pl.BlockSpec((1, tm, LANES), row),
                  pl.BlockSpec((1, tm, d), row),
                  pl.BlockSpec((1, 1, d), lambda i, j: (i, 0, 0)),
                  pl.BlockSpec((1, d), lambda i, j: (0, 0))],
        out_specs=pl.BlockSpec((1, tm, d), row),
        out_shape=jax.ShapeDtypeStruct((b, s, d), F32),
        scratch_shapes=[pltpu.VMEM((2, TOP_K, tm, d), F32),
                        pltpu.SMEM((tm * TOP_K,), I32),
                        pltpu.SMEM((tm * TOP_K,), I32),
                        pltpu.SemaphoreType.DMA(()),
                        pltpu.SemaphoreType.DMA((2,))],
        compiler_params=_params(("arbitrary", "arbitrary")),
        name="moe_combine",
    )(slot_tiles, yb, rp, x1, g2, fg)


def _permute_w_in(w_in):
    d = w_in.shape[0]
    g0 = 4 * M_WIDTH
    a0 = g0 + 2 * M_HEADS
    pad = jnp.zeros((d, LANES - 2 * M_HEADS), w_in.dtype)
    return jnp.concatenate([w_in[:, :g0], w_in[:, a0:], w_in[:, g0:a0], pad], axis=1).astype(BF16)


def kernel(x, c, positions, w_ada, b_ada, norm1_g, w_in, conv_w, conv_b, b_igate, b_fgate, mlstm_norm_g,
           w_out, norm2_g, w_router, b_router, w_gate, b_gate, w_up, b_up, w_down, b_down, final_norm_g):
    b, s, d = x.shape
    depth = w_ada.shape[0]
    t = b * s
    tm = min(MOE_TM, s)
    n_slots = (t * TOP_K // EXPERT_R + N_EXPERTS) * EXPERT_R
    n_blocks = n_slots // EXPERT_R

    mod = _modulation(c, w_ada, b_ada)
    cos_t, sin_t = _rope_tables(positions)

    for l in range(depth):
        sh1, sc1, g1, sh2, sc2, g2 = [mod[l, :, i] for i in range(6)]
        gate_b = jnp.zeros((1, LANES), F32).at[0, :M_HEADS].set(b_igate[l]).at[0, M_HEADS:2 * M_HEADS].set(b_fgate[l])
        zqk, zv, zo, za, vt, zg, kmean = _in_projection(
            x, norm1_g[l].reshape(1, d), sc1, sh1, _permute_w_in(w_in[l]), conv_w[l],
            conv_b[l].reshape(1, -1), gate_b, cos_t, sin_t)
        hm = _mlstm(zqk, zv, zo, zg, mlstm_norm_g[l].reshape(1, -1))
        ha = _moba(za, vt, kmean.reshape(b, s // MOBA_BLOCK, A_WIDTH))

        wr = jnp.zeros((d, LANES), F32).at[:, :N_EXPERTS].set(w_router[l])
        wr_hi = wr.astype(BF16)
        wr_lo = (wr - wr_hi.astype(F32)).astype(BF16)
        b_r = jnp.full((1, LANES), NEG_BIG, F32).at[0, :N_EXPERTS].set(b_router[l])
        x1, h2p, ri, rp, cnt = _out_projection(
            hm, ha, w_out[l].astype(BF16), x, g1, norm2_g[l].reshape(1, d), sc2, sh2, wr_hi, wr_lo, b_r)

        counts = cnt[0, :N_EXPERTS].astype(I32)
        nblk = (counts + EXPERT_R - 1) // EXPERT_R
        blk_end = jnp.cumsum(nblk)
        blk_start = blk_end - nblk
        n_used = blk_end[-1:]
        block_i = jnp.minimum(jnp.arange(n_blocks, dtype=I32), n_used[0] - 1)
        block_e = jnp.minimum(jnp.sum((blk_end[None, :] <= block_i[:, None]).astype(I32), axis=1), N_EXPERTS - 1)
        eid = ri[..., :TOP_K].reshape(t, TOP_K)
        rank = ri[..., TOP_K:2 * TOP_K].reshape(t, TOP_K)
        slot = blk_start[eid] * EXPERT_R + rank
        slot_tiles = slot.reshape(t // tm, tm * TOP_K)

        all_blocks = jnp.arange(n_blocks, dtype=I32)
        cand = jnp.concatenate([jnp.where(nblk > 0, blk_end - 1, -1), jnp.where(all_blocks >= n_used[0], all_blocks, -1)])
        zero_blocks = cand[jnp.argsort(cand < 0)][:2 * N_EXPERTS].astype(I32)
        n_zero = jnp.sum(cand >= 0).astype(I32).reshape(1)
        xs = _scatter_rows(zero_blocks, n_zero, slot_tiles, h2p.reshape(t, d), n_slots)
        yb = _experts(l, block_e, block_i, n_used.astype(I32), xs, w_gate, b_gate, w_up, b_up, w_down, b_down)
        x = _combine(slot_tiles, yb, rp, x1, g2, final_norm_g.reshape(1, d), final=(l == depth - 1))
    return x
```
